```python
import jax, jax.numpy as jnp
from jax import lax
import numpy as np

D_MODEL = 2048
BATCH = 2
SEQ = 8192
DEPTH = 4

GRID_W = 64
CTX_LEN = 256
D_CONV = D_MODEL // 2
D_ATTN = D_MODEL - D_CONV
HEAD_DIM = 64
N_HEADS = D_ATTN // HEAD_DIM
N_KV_HEADS = 4
GQA_GROUP = N_HEADS // N_KV_HEADS
WINDOW = 128
BLOCK = 128
CONV_WIDTH = 3
D_FF = 4 * D_MODEL
ROPE_THETA = 10000.0
ROPE_AXIS_DIM = HEAD_DIM // 2
EPS = 1e-6
N_MOD = 6
KV_START = 3 * D_CONV + D_ATTN
D_IN_PROJ = KV_START + 2 * N_KV_HEADS * HEAD_DIM
SCALE = HEAD_DIM ** -0.5
NEG_INF = -1e30

kernel_name = "hybrid_conv_swa_dit_block"


def rmsnorm(x, g):
    xf = x.astype(jnp.float32)
    y = xf * lax.rsqrt(jnp.mean(xf * xf, axis=-1, keepdims=True) + EPS)
    return (y * g.astype(jnp.float32)).astype(x.dtype)


def modulate(h, shift, scale):
    return h * (1 + scale) + shift


def short_conv(u, w, b):
    n = u.shape[1]
    up = jnp.pad(u, ((0, 0), (1, 1), (0, 0)))
    return up[:, 0:n] * w[0] + up[:, 1:n + 1] * w[1] + up[:, 2:n + 2] * w[2] + b


def gated_conv_mixer(p_conv, w, b):
    bg, cg, h = jnp.split(p_conv, 3, axis=-1)
    return bg * short_conv(cg * h, w, b)


def rope_tables(n_tokens, dtype):
    rows = n_tokens // GRID_W
    row_pos = jnp.repeat(jnp.arange(rows, dtype=jnp.float32), GRID_W)
    col_pos = jnp.tile(jnp.arange(GRID_W, dtype=jnp.float32), rows)
    inv = ROPE_THETA ** (-jnp.arange(0, ROPE_AXIS_DIM, 2, dtype=jnp.float32) / ROPE_AXIS_DIM)
    ang_r = row_pos[:, None] * inv[None, :]
    ang_c = col_pos[:, None] * inv[None, :]
    return (jnp.cos(ang_r)[:, None, :].astype(dtype), jnp.sin(ang_r)[:, None, :].astype(dtype),
            jnp.cos(ang_c)[:, None, :].astype(dtype), jnp.sin(ang_c)[:, None, :].astype(dtype))


def rotate(x, cos, sin):
    x1, x2 = jnp.split(x, 2, axis=-1)
    return jnp.concatenate([x1 * cos - x2 * sin, x2 * cos + x1 * sin], axis=-1)


def rope_2d(x, tabs):
    cr, sr, cc, sc = tabs
    xr, xc = jnp.split(x, 2, axis=-1)
    return jnp.concatenate([rotate(xr, cr, sr), rotate(xc, cc, sc)], axis=-1)


def band_mask(nb, n_tokens):
    n = jnp.arange(nb)[:, None, None]
    r = jnp.arange(BLOCK)[None, :, None]
    j = jnp.arange(3 * BLOCK)[None, None, :]
    q_pos = n * BLOCK + r
    k_pos = (n - 1) * BLOCK + j
    return (jnp.abs(k_pos - q_pos) <= WINDOW) & (k_pos >= 0) & (k_pos < n_tokens)


def latent_window_attention(q, k, v, kc, vc, sink):
    bsz, n_tok = q.shape[0], q.shape[1]
    n_ctx = kc.shape[1]
    nb = n_tok // BLOCK
    qb = q.reshape(bsz, nb, BLOCK, N_KV_HEADS, GQA_GROUP, HEAD_DIM)

    def band(t):
        tb = t.reshape(bsz, nb, BLOCK, N_KV_HEADS, HEAD_DIM)
        tb = jnp.pad(tb, ((0, 0), (1, 1), (0, 0), (0, 0), (0, 0)))
        return jnp.concatenate([tb[:, :-2], tb[:, 1:-1], tb[:, 2:]], axis=2)

    kw, vw = band(k), band(v)
    s_loc = jnp.einsum('bnqhgd,bnkhd->bnhgqk', qb, kw).astype(jnp.float32) * SCALE
    s_loc = jnp.where(band_mask(nb, n_tok)[None, :, None, None], s_loc, NEG_INF)
    s_ctx = jnp.einsum('bnqhgd,bchd->bnhgqc', qb, kc).astype(jnp.float32) * SCALE
    snk = jnp.broadcast_to(sink.astype(jnp.float32).reshape(1, 1, N_KV_HEADS, GQA_GROUP, 1, 1),
                           s_loc.shape[:-1] + (1,))
    p = jax.nn.softmax(jnp.concatenate([s_loc, s_ctx, snk], axis=-1), axis=-1).astype(v.dtype)
    nk = 3 * BLOCK
    o = (jnp.einsum('bnhgqk,bnkhd->bnqhgd', p[..., :nk], vw)
         + jnp.einsum('bnhgqc,bchd->bnqhgd', p[..., nk:nk + n_ctx], vc))
    return o.reshape(bsz, n_tok, D_ATTN)


def context_attention(qc, kc, vc, sink):
    bsz, n_ctx = qc.shape[0], qc.shape[1]
    qg = qc.reshape(bsz, n_ctx, N_KV_HEADS, GQA_GROUP, HEAD_DIM)
    s = jnp.einsum('blhgd,bchd->bhglc', qg, kc).astype(jnp.float32) * SCALE
    snk = jnp.broadcast_to(sink.astype(jnp.float32).reshape(1, N_KV_HEADS, GQA_GROUP, 1, 1),
                           s.shape[:-1] + (1,))
    p = jax.nn.softmax(jnp.concatenate([s, snk], axis=-1), axis=-1).astype(vc.dtype)
    o = jnp.einsum('bhglc,bchd->blhgd', p[..., :n_ctx], vc)
    return o.reshape(bsz, n_ctx, D_ATTN)


def mixer_merge(conv_out, attn_out, g_oc, g_oa, w_out):
    return jnp.concatenate([rmsnorm(conv_out, g_oc), rmsnorm(attn_out, g_oa)], axis=-1) @ w_out


def sq_relu_mlp(h, w1, w2):
    return jnp.square(jax.nn.relu(h @ w1)) @ w2


def setup_inputs(seed: int = 0) -> dict:
    key = jax.random.key(seed)
    ks = jax.random.split(key, 18)

    def nrm(k, shape, s):
        return jax.random.normal(k, shape, jnp.float32) * s

    return {
        "x": nrm(ks[0], (BATCH, SEQ, D_MODEL), 1.0),
        "c": nrm(ks[1], (BATCH, D_MODEL), 1.0),
        "ctx": nrm(ks[2], (BATCH, CTX_LEN, D_MODEL), 1.0),
        "c_ctx": nrm(ks[3], (D_MODEL,), 1.0),
        "w_ada": nrm(ks[4], (DEPTH, D_MODEL, N_MOD * D_MODEL), 0.5 * D_MODEL ** -0.5),
        "b_ada": nrm(ks[5], (DEPTH, N_MOD * D_MODEL), 0.02),
        "g_norm1": 1.0 + nrm(ks[6], (DEPTH, D_MODEL), 0.02),
        "g_norm2": 1.0 + nrm(ks[7], (DEPTH, D_MODEL), 0.02),
        "w_in": nrm(ks[8], (DEPTH, D_MODEL, D_IN_PROJ), D_MODEL ** -0.5),
        "conv_w": nrm(ks[9], (DEPTH, CONV_WIDTH, D_CONV), CONV_WIDTH ** -0.5),
        "conv_b": nrm(ks[10], (DEPTH, D_CONV), 0.02),
        "sink": nrm(ks[11], (DEPTH, N_HEADS), 0.5),
        "g_out_conv": 1.0 + nrm(ks[12], (DEPTH, D_CONV), 0.02),
        "g_out_attn": 1.0 + nrm(ks[13], (DEPTH, D_ATTN), 0.02),
        "w_out": nrm(ks[14], (DEPTH, D_MODEL, D_MODEL), D_MODEL ** -0.5),
        "w_mlp1": nrm(ks[15], (DEPTH, D_MODEL, D_FF), D_MODEL ** -0.5),
        "w_mlp2": nrm(ks[16], (DEPTH, D_FF, D_MODEL), D_FF ** -0.5),
        "g_final": 1.0 + nrm(ks[17], (D_MODEL,), 0.02),
    }


def reference(x, c, ctx, c_ctx, w_ada, b_ada, g_norm1, g_norm2, w_in, conv_w, conv_b, sink,
              g_out_conv, g_out_attn, w_out, w_mlp1, w_mlp2, g_final):
    bsz, n_tok, _ = x.shape
    n_ctx = ctx.shape[1]
    tabs = rope_tables(n_tok, x.dtype)
    sc = jax.nn.silu(c)
    scc = jax.nn.silu(c_ctx)
    for i in range(DEPTH):
        last = i == DEPTH - 1
        m = jnp.split((sc @ w_ada[i] + b_ada[i])[:, None, :], N_MOD, axis=-1)
        mc = jnp.split(scc @ w_ada[i] + b_ada[i], N_MOD, axis=-1)

        h = modulate(rmsnorm(x, g_norm1[i]), m[0], m[1])
        hc = modulate(rmsnorm(ctx, g_norm1[i]), mc[0], mc[1])
        p = h @ w_in[i]
        p_conv = p[..., :3 * D_CONV]
        q = p[..., 3 * D_CONV:KV_START].reshape(bsz, n_tok, N_HEADS, HEAD_DIM)
        k, v = jnp.split(p[..., KV_START:], 2, axis=-1)
        q = rope_2d(q, tabs)
        k = rope_2d(k.reshape(bsz, n_tok, N_KV_HEADS, HEAD_DIM), tabs)
        v = v.reshape(bsz, n_tok, N_KV_HEADS, HEAD_DIM)
        kc, vc = jnp.split(hc @ w_in[i][:, KV_START:], 2, axis=-1)
        kc = kc.reshape(bsz, n_ctx, N_KV_HEADS, HEAD_DIM)
        vc = vc.reshape(bsz, n_ctx, N_KV_HEADS, HEAD_DIM)

        conv_out = gated_conv_mixer(p_conv, conv_w[i], conv_b[i])
        attn_out = latent_window_attention(q, k, v, kc, vc, sink[i])
        x = x + m[2] * mixer_merge(conv_out, attn_out, g_out_conv[i], g_out_attn[i], w_out[i])

        if not last:
            pc = hc @ w_in[i][:, :KV_START]
            ctx_conv = gated_conv_mixer(pc[..., :3 * D_CONV], conv_w[i], conv_b[i])
            qc = pc[..., 3 * D_CONV:].reshape(bsz, n_ctx, N_HEADS, HEAD_DIM)
            ctx_attn = context_attention(qc, kc, vc, sink[i])
            ctx = ctx + mc[2] * mixer_merge(ctx_conv, ctx_attn, g_out_conv[i], g_out_attn[i], w_out[i])

        x = x + m[5] * sq_relu_mlp(modulate(rmsnorm(x, g_norm2[i]), m[3], m[4]), w_mlp1[i], w_mlp2[i])
        if not last:
            ctx = ctx + mc[5] * sq_relu_mlp(modulate(rmsnorm(ctx, g_norm2[i]), mc[3], mc[4]),
                                            w_mlp1[i], w_mlp2[i])
    return rmsnorm(x, g_final)
```

```python
import functools

import jax
import jax.numpy as jnp
from jax import lax
from jax.experimental import pallas as pl
from jax.experimental.pallas import tpu as pltpu

D_MODEL = 2048
D_CONV = D_MODEL // 2
D_ATTN = D_MODEL - D_CONV
HEAD_DIM = 64
N_HEADS = D_ATTN // HEAD_DIM
N_KV_HEADS = 4
GQA_GROUP = N_HEADS // N_KV_HEADS
D_KV = N_KV_HEADS * HEAD_DIM
GRID_W = 64
WINDOW = 128
BLOCK = 128
D_FF = 4 * D_MODEL
ROPE_THETA = 10000.0
ROPE_AXIS_DIM = HEAD_DIM // 2
ROPE_HALF = ROPE_AXIS_DIM // 2
EPS = 1e-6
N_MOD = 6
KV_START = 3 * D_CONV + D_ATTN
D_IN_PROJ = KV_START + 2 * D_KV
SCALE = HEAD_DIM ** -0.5
NEG_INF = -1e30

LANES = 128
BF16_SUBLANES = 16
MOD_ROWS = 8
TM = 512
TN_IN = 512
TF = 512
TN_ADA = 1024
VMEM_LIMIT = 56 * 1024 * 1024

F32 = jnp.float32
BF16 = jnp.bfloat16


def _params(sem):
    return pltpu.CompilerParams(dimension_semantics=sem, vmem_limit_bytes=VMEM_LIMIT)


def _rms(v):
    return v * lax.rsqrt(jnp.mean(v * v, axis=-1, keepdims=True) + EPS)


def _dot(a, b):
    return jnp.dot(a, b, preferred_element_type=F32)


def _dot_nt(a, b):
    return lax.dot_general(a, b, (((1,), (1,)), ((), ())), preferred_element_type=F32)


def _ada_kernel(c_ref, w_ref, b_ref, o_ref):
    c = c_ref[...]
    s = (c * jax.nn.sigmoid(c)).astype(BF16)
    o_ref[0] = _dot(s, w_ref[0].astype(BF16)) + b_ref[0]


def _ada(cc, w_ada, b_ada):
    depth = w_ada.shape[0]
    n_out = N_MOD * D_MODEL
    return pl.pallas_call(
        _ada_kernel,
        grid=(depth, n_out // TN_ADA),
        in_specs=[
            pl.BlockSpec((MOD_ROWS, D_MODEL), lambda l, j: (0, 0)),
            pl.BlockSpec((1, D_MODEL, TN_ADA), lambda l, j: (l, 0, j)),
            pl.BlockSpec((1, 1, TN_ADA), lambda l, j: (l, 0, j)),
        ],
        out_specs=pl.BlockSpec((1, MOD_ROWS, TN_ADA), lambda l, j: (l, 0, j)),
        out_shape=jax.ShapeDtypeStruct((depth, MOD_ROWS, n_out), F32),
        compiler_params=_params(("arbitrary", "arbitrary")),
        name="adaln",
    )(cc, w_ada, b_ada.reshape(depth, 1, n_out))


def _rope(a, cos, sin_hi, sin_lo):
    outs = []
    for c in range(a.shape[1] // LANES):
        v = a[:, c * LANES:(c + 1) * LANES]
        outs.append(v * cos + pltpu.roll(v, ROPE_HALF, 1) * sin_hi
                    + pltpu.roll(v, LANES - ROPE_HALF, 1) * sin_lo)
    return outs


def _inproj_kernel(x_ref, g_ref, m_ref, w_ref, cos_ref, shi_ref, slo_ref, o_ref, h_scr):
    j = pl.program_id(1)

    @pl.when(j == 0)
    def _():
        y = _rms(x_ref[...]) * g_ref[...]
        h_scr[...] = (y * (1 + m_ref[0, 1:2, :]) + m_ref[0, 0:1, :]).astype(BF16)

    acc = _dot(h_scr[...], w_ref[...])
    q_first = (3 * D_CONV) // TN_IN
    kv_tile = KV_START // TN_IN

    @pl.when(j < q_first)
    def _():
        o_ref[...] = acc.astype(BF16)

    @pl.when((j >= q_first) & (j < kv_tile))
    def _():
        outs = _rope(acc, cos_ref[...], shi_ref[...], slo_ref[...])
        for c, v in enumerate(outs):
            o_ref[:, c * LANES:(c + 1) * LANES] = (v * SCALE).astype(BF16)

    @pl.when(j == kv_tile)
    def _():
        outs = _rope(acc[:, :D_KV], cos_ref[...], shi_ref[...], slo_ref[...])
        for c, v in enumerate(outs):
            o_ref[:, c * LANES:(c + 1) * LANES] = v.astype(BF16)
        o_ref[:, D_KV:] = acc[:, D_KV:].astype(BF16)


def _inproj(x, g, mods, w, tabs, geo):
    n_tiles, tpb, n_lat_tiles, n_groups = geo
    group = lambda i, j: (jnp.minimum(i // tpb, n_groups - 1), 0, 0)
    tab_idx = lambda i, j: (jnp.where(i < n_lat_tiles, i % tpb, tpb), 0)
    return pl.pallas_call(
        _inproj_kernel,
        grid=(n_tiles, D_IN_PROJ // TN_IN),
        in_specs=[
            pl.BlockSpec((TM, D_MODEL), lambda i, j: (i, 0)),
            pl.BlockSpec((1, D_MODEL), lambda i, j: (0, 0)),
            pl.BlockSpec((1, N_MOD, D_MODEL), group),
            pl.BlockSpec((D_MODEL, TN_IN), lambda i, j: (0, j)),
            pl.BlockSpec((TM, LANES), tab_idx),
            pl.BlockSpec((TM, LANES), tab_idx),
            pl.BlockSpec((TM, LANES), tab_idx),
        ],
        out_specs=pl.BlockSpec((TM, TN_IN), lambda i, j: (i, j)),
        out_shape=jax.ShapeDtypeStruct((n_tiles * TM, D_IN_PROJ), BF16),
        scratch_shapes=[pltpu.VMEM((TM, D_MODEL), BF16)],
        compiler_params=_params(("arbitrary", "arbitrary")),
        name="inproj",
    )(x, g, mods, w, *tabs)


def _stack_heads(q_blk):
    return jnp.concatenate([q_blk[:, g * HEAD_DIM:(g + 1) * HEAD_DIM] for g in range(GQA_GROUP)], axis=0)


def _attend(qs, parts, snk):
    scores = []
    for k, _, mask in parts:
        s = _dot_nt(qs, k)
        if mask is not None:
            s = jnp.where(mask, s, NEG_INF)
        scores.append(s)
    m = snk
    for s in scores:
        m = jnp.maximum(m, jnp.max(s, axis=-1, keepdims=True))
    denom = jnp.exp(snk - m)
    o = None
    for s, (_, v, _) in zip(scores, parts):
        e = jnp.exp(s - m)
        denom = denom + jnp.sum(e, axis=-1, keepdims=True)
        pv = _dot(e.astype(BF16), v)
        o = pv if o is None else o + pv
    return o / denom


def _sink_col(sink_ref, kvh, rows):
    return jnp.concatenate(
        [jnp.full((rows, 1), sink_ref[kvh * GQA_GROUP + g], F32) for g in range(GQA_GROUP)], axis=0)


def _mixer_kernel(x_ref, bg_ref, cg_ref, hh_ref, cgp_ref, hhp_ref, cgn_ref, hhn_ref, q_ref,
                  kp_ref, kc_ref, kn_ref, vp_ref, vc_ref, vn_ref, ck_ref, cv_ref, *rest,
                  n_lat_tiles, tpb, n_ctx, n_seq_ctx):
    ctx_refs = rest[:2 * n_seq_ctx]
    (wout_ref, cw_ref, cb_ref, goc_ref, goa_ref, sink_ref, m_ref, o_ref, attn_scr) = rest[2 * n_seq_ctx:]
    i = pl.program_id(0)
    is_lat = i < n_lat_tiles
    tib = i % tpb

    u = cg_ref[...].astype(F32) * hh_ref[...].astype(F32)
    last = BF16_SUBLANES - 1
    prev_edge = cgp_ref[last:last + 1, :].astype(F32) * hhp_ref[last:last + 1, :].astype(F32)
    next_edge = cgn_ref[0:1, :].astype(F32) * hhn_ref[0:1, :].astype(F32)
    prev_edge = jnp.where(is_lat & (tib > 0), prev_edge, 0.0)
    next_edge = jnp.where(is_lat & (tib < tpb - 1), next_edge, 0.0)
    r = lax.broadcasted_iota(jnp.int32, (TM, 1), 0)
    not_lat = jnp.logical_not(is_lat)
    seq_start, seq_end = r == 0, r == TM - 1
    for sb in range(1, n_seq_ctx):
        edge = jnp.where(is_lat, -1, sb * n_ctx)
        seq_start = seq_start | (r == edge)
        seq_end = seq_end | (r == edge - 1)
    u_prev = jnp.where(seq_start, prev_edge, pltpu.roll(u, 1, 0))
    u_next = jnp.where(seq_end, next_edge, pltpu.roll(u, TM - 1, 0))
    conv = u_prev * cw_ref[0:1, :] + u * cw_ref[1:2, :] + u_next * cw_ref[2:3, :] + cb_ref[...]
    conv_n = (_rms(bg_ref[...].astype(F32) * conv) * goc_ref[...]).astype(BF16)

    @pl.when(is_lat)
    def _():
        k_ext = jnp.concatenate([kp_ref[...], kc_ref[...], kn_ref[...]], axis=0)
        v_ext = jnp.concatenate([vp_ref[...], vc_ref[...], vn_ref[...]], axis=0)
        rr = lax.broadcasted_iota(jnp.int32, (GQA_GROUP * BLOCK, 3 * BLOCK), 0) % BLOCK
        jj = lax.broadcasted_iota(jnp.int32, (GQA_GROUP * BLOCK, 3 * BLOCK), 1)
        band = (jj >= rr) & (jj <= rr + 2 * WINDOW)
        for qb in range(TM // BLOCK):
            mask = band
            if qb == 0:
                mask = mask & (jj >= jnp.where(tib > 0, 0, BLOCK))
            if qb == TM // BLOCK - 1:
                mask = mask & (jj < jnp.where(tib < tpb - 1, 3 * BLOCK, 2 * BLOCK))
            rows = slice(qb * BLOCK, (qb + 1) * BLOCK)
            win = slice(qb * BLOCK, qb * BLOCK + 3 * BLOCK)
            for kvh in range(N_KV_HEADS):
                hd = slice(kvh * HEAD_DIM, (kvh + 1) * HEAD_DIM)
                qs = _stack_heads(q_ref[rows, kvh * GQA_GROUP * HEAD_DIM:(kvh + 1) * GQA_GROUP * HEAD_DIM])
                o = _attend(qs,
                            [(k_ext[win, hd], v_ext[win, hd], mask),
                             (ck_ref[:, hd], cv_ref[:, hd], None)],
                            _sink_col(sink_ref, kvh, BLOCK))
                for g in range(GQA_GROUP):
                    c0 = (kvh * GQA_GROUP + g) * HEAD_DIM
                    attn_scr[rows, c0:c0 + HEAD_DIM] = o[g * BLOCK:(g + 1) * BLOCK]

    @pl.when(not_lat)
    def _():
        for sb in range(n_seq_ctx):
            rows = slice(sb * n_ctx, (sb + 1) * n_ctx)
            k_all, v_all = ctx_refs[2 * sb], ctx_refs[2 * sb + 1]
            for kvh in range(N_KV_HEADS):
                hd = slice(kvh * HEAD_DIM, (kvh + 1) * HEAD_DIM)
                qs = _stack_heads(q_ref[rows, kvh * GQA_GROUP * HEAD_DIM:(kvh + 1) * GQA_GROUP * HEAD_DIM])
                o = _attend(qs, [(k_all[:, hd], v_all[:, hd], None)], _sink_col(sink_ref, kvh, n_ctx))
                for g in range(GQA_GROUP):
                    c0 = (kvh * GQA_GROUP + g) * HEAD_DIM
                    attn_scr[rows, c0:c0 + HEAD_DIM] = o[g * n_ctx:(g + 1) * n_ctx]

    attn_n = (_rms(attn_scr[...]) * goa_ref[...]).astype(BF16)

    y = _dot(conv_n, wout_ref[:D_CONV, :]) + _dot(attn_n, wout_ref[D_CONV:, :])
    o_ref[...] = x_ref[...] + m_ref[0, 2:3, :] * y


def _mixer(x, p, w_out, conv_w, conv_b, g_oc, g_oa, sink, mods, geo, n_ctx, n_out_tiles):
    n_tiles, tpb, n_lat_tiles, n_groups = geo
    n_seq_ctx = TM // n_ctx
    halo_blocks = TM // BF16_SUBLANES
    kv_blocks = TM // BLOCK
    n_halo, n_kvb = n_tiles * halo_blocks, n_tiles * kv_blocks
    col_cg, col_hh, col_q = 1, 2, 3
    col_k, col_v = KV_START // D_KV, KV_START // D_KV + 1
    ctx_blk0 = n_lat_tiles * TM // n_ctx
    seq_of = lambda i: jnp.minimum(i // tpb, n_seq_ctx - 1)
    group = lambda i: (jnp.minimum(i // tpb, n_groups - 1), 0, 0)

    def halo(col, nxt):
        if nxt:
            return pl.BlockSpec((BF16_SUBLANES, D_CONV),
                                lambda i: (jnp.minimum((i + 1) * halo_blocks, n_halo - 1), col))
        return pl.BlockSpec((BF16_SUBLANES, D_CONV), lambda i: (jnp.maximum(i * halo_blocks - 1, 0), col))

    def kv(col, where):
        if where == 0:
            return pl.BlockSpec((TM, D_KV), lambda i: (i, col))
        if where < 0:
            return pl.BlockSpec((BLOCK, D_KV), lambda i: (jnp.maximum(i * kv_blocks - 1, 0), col))
        return pl.BlockSpec((BLOCK, D_KV), lambda i: (jnp.minimum((i + 1) * kv_blocks, n_kvb - 1), col))

    in_specs = [
        pl.BlockSpec((TM, D_MODEL), lambda i: (i, 0)),
        pl.BlockSpec((TM, D_CONV), lambda i: (i, 0)),
        pl.BlockSpec((TM, D_CONV), lambda i: (i, col_cg)),
        pl.BlockSpec((TM, D_CONV), lambda i: (i, col_hh)),
        halo(col_cg, False), halo(col_hh, False), halo(col_cg, True), halo(col_hh, True),
        pl.BlockSpec((TM, D_ATTN), lambda i: (i, col_q)),
        kv(col_k, -1), kv(col_k, 0), kv(col_k, 1),
        kv(col_v, -1), kv(col_v, 0), kv(col_v, 1),
        pl.BlockSpec((n_ctx, D_KV), lambda i: (ctx_blk0 + seq_of(i), col_k)),
        pl.BlockSpec((n_ctx, D_KV), lambda i: (ctx_blk0 + seq_of(i), col_v)),
    ]
    args = [x, p, p, p, p, p, p, p, p, p, p, p, p, p, p, p, p]
    for sb in range(n_seq_ctx):
        in_specs += [pl.BlockSpec((n_ctx, D_KV), functools.partial(lambda i, s: (ctx_blk0 + s, col_k), s=sb)),
                     pl.BlockSpec((n_ctx, D_KV), functools.partial(lambda i, s: (ctx_blk0 + s, col_v), s=sb))]
        args += [p, p]
    in_specs += [
        pl.BlockSpec((D_MODEL, D_MODEL), lambda i: (0, 0), pipeline_mode=pl.Buffered(1)),
        pl.BlockSpec((3, D_CONV), lambda i: (0, 0)),
        pl.BlockSpec((1, D_CONV), lambda i: (0, 0)),
        pl.BlockSpec((1, D_CONV), lambda i: (0, 0)),
        pl.BlockSpec((1, D_ATTN), lambda i: (0, 0)),
        pl.BlockSpec(memory_space=pltpu.SMEM),
        pl.BlockSpec((1, N_MOD, D_MODEL), group),
    ]
    args += [w_out, conv_w, conv_b, g_oc, g_oa, sink, mods]
    kern = functools.partial(_mixer_kernel, n_lat_tiles=n_lat_tiles, tpb=tpb, n_ctx=n_ctx, n_seq_ctx=n_seq_ctx)
    return pl.pallas_call(
        kern,
        grid=(n_out_tiles,),
        in_specs=in_specs,
        out_specs=pl.BlockSpec((TM, D_MODEL), lambda i: (i, 0)),
        out_shape=jax.ShapeDtypeStruct((n_out_tiles * TM, D_MODEL), F32),
        scratch_shapes=[pltpu.VMEM((TM, D_ATTN), F32)],
        compiler_params=_params(("arbitrary",)),
        name="mixer",
    )(*args)


def _mlp_kernel(x_ref, g_ref, m_ref, w1_ref, w2_ref, gf_ref, o_ref, h_scr, *, final):
    f = pl.program_id(1)

    @pl.when(f == 0)
    def _():
        y = _rms(x_ref[...]) * g_ref[...]
        h_scr[...] = (y * (1 + m_ref[0, 4:5, :]) + m_ref[0, 3:4, :]).astype(BF16)

    hid = jnp.square(jnp.maximum(_dot(h_scr[...], w1_ref[...]), 0.0)).astype(BF16)
    part = _dot(hid, w2_ref[...])

    @pl.when(f == 0)
    def _():
        o_ref[...] = part

    @pl.when(f > 0)
    def _():
        o_ref[...] += part

    @pl.when(f == pl.num_programs(1) - 1)
    def _():
        x_new = x_ref[...] + m_ref[0, 5:6, :] * o_ref[...]
        if final:
            x_new = _rms(x_new) * gf_ref[...]
        o_ref[...] = x_new


def _mlp(x, g, mods, w1, w2, g_final, geo, n_out_tiles, final):
    n_tiles, tpb, n_lat_tiles, n_groups = geo
    group = lambda i, f: (jnp.minimum(i // tpb, n_groups - 1), 0, 0)
    return pl.pallas_call(
        functools.partial(_mlp_kernel, final=final),
        grid=(n_out_tiles, D_FF // TF),
        in_specs=[
            pl.BlockSpec((TM, D_MODEL), lambda i, f: (i, 0)),
            pl.BlockSpec((1, D_MODEL), lambda i, f: (0, 0)),
            pl.BlockSpec((1, N_MOD, D_MODEL), group),
            pl.BlockSpec((D_MODEL, TF), lambda i, f: (0, f)),
            pl.BlockSpec((TF, D_MODEL), lambda i, f: (f, 0)),
            pl.BlockSpec((1, D_MODEL), lambda i, f: (0, 0)),
        ],
        out_specs=pl.BlockSpec((TM, D_MODEL), lambda i, f: (i, 0)),
        out_shape=jax.ShapeDtypeStruct((n_out_tiles * TM, D_MODEL), F32),
        scratch_shapes=[pltpu.VMEM((TM, D_MODEL), BF16)],
        compiler_params=_params(("arbitrary", "arbitrary")),
        name="mlp",
    )(x, g, mods, w1, w2, g_final)


def _rope_tables(n_tok):
    rows = n_tok // GRID_W
    row_pos = jnp.repeat(jnp.arange(rows, dtype=F32), GRID_W)
    col_pos = jnp.tile(jnp.arange(GRID_W, dtype=F32), rows)
    inv = ROPE_THETA ** (-jnp.arange(0, ROPE_AXIS_DIM, 2, dtype=F32) / ROPE_AXIS_DIM)
    ang_r = row_pos[:, None] * inv[None, :]
    ang_c = col_pos[:, None] * inv[None, :]
    zeros = jnp.zeros_like(ang_r)
    cos_h = jnp.concatenate([jnp.cos(ang_r)] * 2 + [jnp.cos(ang_c)] * 2, axis=-1)
    hi_h = jnp.concatenate([zeros, jnp.sin(ang_r), zeros, jnp.sin(ang_c)], axis=-1)
    lo_h = jnp.concatenate([-jnp.sin(ang_r), zeros, -jnp.sin(ang_c), zeros], axis=-1)
    reps = LANES // HEAD_DIM
    pad = lambda t, fill: jnp.concatenate([jnp.tile(t, (1, reps)), jnp.full((TM, LANES), fill, F32)], axis=0)
    return pad(cos_h, 1.0), pad(hi_h, 0.0), pad(lo_h, 0.0)


def kernel(x, c, ctx, c_ctx, w_ada, b_ada, g_norm1, g_norm2, w_in, conv_w, conv_b, sink,
           g_out_conv, g_out_attn, w_out, w_mlp1, w_mlp2, g_final):
    bsz, n_tok, d = x.shape
    n_ctx = ctx.shape[1]
    depth = w_ada.shape[0]
    assert d == D_MODEL and n_tok % TM == 0 and n_tok % GRID_W == 0
    assert bsz * n_ctx == TM and n_ctx % BLOCK == 0 and bsz + 1 <= MOD_ROWS
    tpb = n_tok // TM
    n_lat_tiles = bsz * tpb
    n_tiles = n_lat_tiles + 1
    geo = (n_tiles, tpb, n_lat_tiles, bsz + 1)

    cc = jnp.concatenate([c, c_ctx[None, :], jnp.zeros((MOD_ROWS - bsz - 1, d), F32)], axis=0)
    mods = _ada(cc, w_ada, b_ada).reshape(depth, MOD_ROWS, N_MOD, d)
    tabs = _rope_tables(n_tok)
    xs = jnp.concatenate([x.reshape(bsz * n_tok, d), ctx.reshape(bsz * n_ctx, d)], axis=0)
    row = lambda v: v.reshape(1, -1)

    for i in range(depth):
        last = i == depth - 1
        n_out = n_lat_tiles if last else n_tiles
        p = _inproj(xs, row(g_norm1[i]), mods[i], w_in[i].astype(BF16), tabs, geo)
        xs = _mixer(xs, p, w_out[i].astype(BF16), conv_w[i], row(conv_b[i]), row(g_out_conv[i]),
                    row(g_out_attn[i]), sink[i], mods[i], geo, n_ctx, n_out)
        xs = _mlp(xs, row(g_norm2[i]), mods[i], w_mlp1[i].astype(BF16), w_mlp2[i].astype(BF16),
                  row(g_final), geo, n_out, last)
    return xs.reshape(bsz, n_tok, d)
```

```python
import functools

import jax
import jax.numpy as jnp
from jax import lax
from jax.experimental import pallas as pl
from jax.experimental.pallas import tpu as pltpu

D_MODEL = 2048
D_CONV = D_MODEL // 2
D_ATTN = D_MODEL - D_CONV
HEAD_DIM = 64
N_HEADS = D_ATTN // HEAD_DIM
N_KV_HEADS = 4
GQA_GROUP = N_HEADS // N_KV_HEADS
D_KV = N_KV_HEADS * HEAD_DIM
GRID_W = 64
WINDOW = 128
BLOCK = 128
D_FF = 4 * D_MODEL
ROPE_THETA = 10000.0
ROPE_AXIS_DIM = HEAD_DIM // 2
ROPE_HALF = ROPE_AXIS_DIM // 2
EPS = 1e-6
N_MOD = 6
KV_START = 3 * D_CONV + D_ATTN
D_IN_PROJ = KV_START + 2 * D_KV
SCALE = HEAD_DIM ** -0.5
NEG_INF = -1e30

LANES = 128
BF16_SUBLANES = 16
MOD_ROWS = 8
TM = 512
TN_IN = 512
TF = 1024
TN_ADA = 1024
VMEM_LIMIT = 56 * 1024 * 1024

F32 = jnp.float32
BF16 = jnp.bfloat16


def _params(sem):
    return pltpu.CompilerParams(dimension_semantics=sem, vmem_limit_bytes=VMEM_LIMIT)


def _rms(v):
    return v * lax.rsqrt(jnp.mean(v * v, axis=-1, keepdims=True) + EPS)


def _dot(a, b):
    return jnp.dot(a, b, preferred_element_type=F32)


def _dot_nt(a, b):
    return lax.dot_general(a, b, (((1,), (1,)), ((), ())), preferred_element_type=F32)


def _ada_kernel(c_ref, w_ref, b_ref, o_ref):
    c = c_ref[...]
    s = (c * jax.nn.sigmoid(c)).astype(BF16)
    o_ref[0] = _dot(s, w_ref[0].astype(BF16)) + b_ref[0]


def _ada(cc, w_ada, b_ada):
    depth = w_ada.shape[0]
    n_out = N_MOD * D_MODEL
    return pl.pallas_call(
        _ada_kernel,
        grid=(depth, n_out // TN_ADA),
        in_specs=[
            pl.BlockSpec((MOD_ROWS, D_MODEL), lambda l, j: (0, 0)),
            pl.BlockSpec((1, D_MODEL, TN_ADA), lambda l, j: (l, 0, j)),
            pl.BlockSpec((1, 1, TN_ADA), lambda l, j: (l, 0, j)),
        ],
        out_specs=pl.BlockSpec((1, MOD_ROWS, TN_ADA), lambda l, j: (l, 0, j)),
        out_shape=jax.ShapeDtypeStruct((depth, MOD_ROWS, n_out), F32),
        compiler_params=_params(("arbitrary", "arbitrary")),
        name="adaln",
    )(cc, w_ada, b_ada.reshape(depth, 1, n_out))


def _rope(a, cos, sin_hi, sin_lo):
    outs = []
    for c in range(a.shape[1] // LANES):
        v = a[:, c * LANES:(c + 1) * LANES]
        outs.append(v * cos + pltpu.roll(v, ROPE_HALF, 1) * sin_hi
                    + pltpu.roll(v, LANES - ROPE_HALF, 1) * sin_lo)
    return outs


def _norm_mod(x, g, m, shift_row, scale_row):
    y = _rms(x) * g
    return (y * (1 + m[scale_row:scale_row + 1, :]) + m[shift_row:shift_row + 1, :]).astype(BF16)


def _inproj_body(h_ref, w_ref, cos_ref, shi_ref, slo_ref, o_ref):
    h = h_ref[...]
    cos, shi, slo = cos_ref[...], shi_ref[...], slo_ref[...]
    q_first = (3 * D_CONV) // TN_IN
    kv_tile = KV_START // TN_IN
    for c in range(D_IN_PROJ // TN_IN):
        cols = slice(c * TN_IN, (c + 1) * TN_IN)
        acc = _dot(h, w_ref[:, cols])
        if c < q_first:
            o_ref[:, cols] = acc.astype(BF16)
        elif c < kv_tile:
            for k, v in enumerate(_rope(acc, cos, shi, slo)):
                o_ref[:, c * TN_IN + k * LANES:c * TN_IN + (k + 1) * LANES] = (v * SCALE).astype(BF16)
        else:
            for k, v in enumerate(_rope(acc[:, :D_KV], cos, shi, slo)):
                o_ref[:, c * TN_IN + k * LANES:c * TN_IN + (k + 1) * LANES] = v.astype(BF16)
            o_ref[:, c * TN_IN + D_KV:(c + 1) * TN_IN] = acc[:, D_KV:].astype(BF16)


def _inproj_kernel(x0_ref, xn_ref, g_ref, m0_ref, mn_ref, w_ref, cos_ref, shi_ref, slo_ref, o_ref,
                   h_even, h_odd):
    i = pl.program_id(0)

    @pl.when(i == 0)
    def _():
        h_even[...] = _norm_mod(x0_ref[...], g_ref[...], m0_ref[0], 0, 1)

    for parity, (h_cur, h_nxt) in enumerate(((h_even, h_odd), (h_odd, h_even))):
        @pl.when(i % 2 == parity)
        def _():
            h_nxt[...] = _norm_mod(xn_ref[...], g_ref[...], mn_ref[0], 0, 1)
            _inproj_body(h_cur, w_ref, cos_ref, shi_ref, slo_ref, o_ref)


def _inproj(x, g, mods, w, tabs, geo):
    n_tiles, tpb, n_lat_tiles, n_groups = geo
    nxt = lambda i: jnp.minimum(i + 1, n_tiles - 1)
    group_nxt = lambda i: (jnp.minimum(nxt(i) // tpb, n_groups - 1), 0, 0)
    tab_idx = lambda i: (jnp.where(i < n_lat_tiles, i % tpb, tpb), 0)
    once = pl.Buffered(1)
    return pl.pallas_call(
        _inproj_kernel,
        grid=(n_tiles,),
        in_specs=[
            pl.BlockSpec((TM, D_MODEL), lambda i: (0, 0), pipeline_mode=once),
            pl.BlockSpec((TM, D_MODEL), lambda i: (nxt(i), 0)),
            pl.BlockSpec((1, D_MODEL), lambda i: (0, 0)),
            pl.BlockSpec((1, N_MOD, D_MODEL), lambda i: (0, 0, 0)),
            pl.BlockSpec((1, N_MOD, D_MODEL), group_nxt),
            pl.BlockSpec((D_MODEL, D_IN_PROJ), lambda i: (0, 0), pipeline_mode=once),
            pl.BlockSpec((TM, LANES), tab_idx),
            pl.BlockSpec((TM, LANES), tab_idx),
            pl.BlockSpec((TM, LANES), tab_idx),
        ],
        out_specs=pl.BlockSpec((TM, D_IN_PROJ), lambda i: (i, 0)),
        out_shape=jax.ShapeDtypeStruct((n_tiles * TM, D_IN_PROJ), BF16),
        scratch_shapes=[pltpu.VMEM((TM, D_MODEL), BF16), pltpu.VMEM((TM, D_MODEL), BF16)],
        compiler_params=_params(("arbitrary",)),
        name="inproj",
    )(x, x, g, mods, mods, w, *tabs)


def _stack_heads(q_blk):
    return jnp.concatenate([q_blk[:, g * HEAD_DIM:(g + 1) * HEAD_DIM] for g in range(GQA_GROUP)], axis=0)


def _attend(qs, parts, snk):
    scores = []
    for k, _, mask in parts:
        s = _dot_nt(qs, k)
        if mask is not None:
            s = jnp.where(mask, s, NEG_INF)
        scores.append(s)
    m = snk
    for s in scores:
        m = jnp.maximum(m, jnp.max(s, axis=-1, keepdims=True))
    denom = jnp.exp(snk - m)
    o = None
    for s, (_, v, _) in zip(scores, parts):
        e = jnp.exp(s - m)
        denom = denom + jnp.sum(e, axis=-1, keepdims=True)
        pv = _dot(e.astype(BF16), v)
        o = pv if o is None else o + pv
    return o / denom


def _sink_col(sink_ref, kvh, rows):
    return jnp.concatenate(
        [jnp.full((rows, 1), sink_ref[kvh * GQA_GROUP + g], F32) for g in range(GQA_GROUP)], axis=0)


def _mixer_kernel(x_ref, bg_ref, cg_ref, hh_ref, cgp_ref, hhp_ref, cgn_ref, hhn_ref, q_ref,
                  kp_ref, kc_ref, kn_ref, vp_ref, vc_ref, vn_ref, ck_ref, cv_ref, *rest,
                  n_lat_tiles, tpb, n_ctx, n_seq_ctx):
    ctx_refs = rest[:2 * n_seq_ctx]
    (wout_ref, cw_ref, cb_ref, goc_ref, goa_ref, sink_ref, m_ref, o_ref, attn_scr) = rest[2 * n_seq_ctx:]
    i = pl.program_id(0)
    is_lat = i < n_lat_tiles
    tib = i % tpb

    u = cg_ref[...].astype(F32) * hh_ref[...].astype(F32)
    last = BF16_SUBLANES - 1
    prev_edge = cgp_ref[last:last + 1, :].astype(F32) * hhp_ref[last:last + 1, :].astype(F32)
    next_edge = cgn_ref[0:1, :].astype(F32) * hhn_ref[0:1, :].astype(F32)
    prev_edge = jnp.where(is_lat & (tib > 0), prev_edge, 0.0)
    next_edge = jnp.where(is_lat & (tib < tpb - 1), next_edge, 0.0)
    r = lax.broadcasted_iota(jnp.int32, (TM, 1), 0)
    not_lat = jnp.logical_not(is_lat)
    seq_start, seq_end = r == 0, r == TM - 1
    for sb in range(1, n_seq_ctx):
        edge = jnp.where(is_lat, -1, sb * n_ctx)
        seq_start = seq_start | (r == edge)
        seq_end = seq_end | (r == edge - 1)
    u_prev = jnp.where(seq_start, prev_edge, pltpu.roll(u, 1, 0))
    u_next = jnp.where(seq_end, next_edge, pltpu.roll(u, TM - 1, 0))
    conv = u_prev * cw_ref[0:1, :] + u * cw_ref[1:2, :] + u_next * cw_ref[2:3, :] + cb_ref[...]
    conv_n = (_rms(bg_ref[...].astype(F32) * conv) * goc_ref[...]).astype(BF16)

    @pl.when(is_lat)
    def _():
        k_ext = jnp.concatenate([kp_ref[...], kc_ref[...], kn_ref[...]], axis=0)
        v_ext = jnp.concatenate([vp_ref[...], vc_ref[...], vn_ref[...]], axis=0)
        rr = lax.broadcasted_iota(jnp.int32, (GQA_GROUP * BLOCK, 3 * BLOCK), 0) % BLOCK
        jj = lax.broadcasted_iota(jnp.int32, (GQA_GROUP * BLOCK, 3 * BLOCK), 1)
        band = (jj >= rr) & (jj <= rr + 2 * WINDOW)
        for qb in range(TM // BLOCK):
            mask = band
            if qb == 0:
                mask = mask & (jj >= jnp.where(tib > 0, 0, BLOCK))
            if qb == TM // BLOCK - 1:
                mask = mask & (jj < jnp.where(tib < tpb - 1, 3 * BLOCK, 2 * BLOCK))
            rows = slice(qb * BLOCK, (qb + 1) * BLOCK)
            win = slice(qb * BLOCK, qb * BLOCK + 3 * BLOCK)
            for kvh in range(N_KV_HEADS):
                hd = slice(kvh * HEAD_DIM, (kvh + 1) * HEAD_DIM)
                qs = _stack_heads(q_ref[rows, kvh * GQA_GROUP * HEAD_DIM:(kvh + 1) * GQA_GROUP * HEAD_DIM])
                o = _attend(qs,
                            [(k_ext[win, hd], v_ext[win, hd], mask),
                             (ck_ref[:, hd], cv_ref[:, hd], None)],
                            _sink_col(sink_ref, kvh, BLOCK))
                for g in range(GQA_GROUP):
                    c0 = (kvh * GQA_GROUP + g) * HEAD_DIM
                    attn_scr[rows, c0:c0 + HEAD_DIM] = o[g * BLOCK:(g + 1) * BLOCK]

    @pl.when(not_lat)
    def _():
        for sb in range(n_seq_ctx):
            rows = slice(sb * n_ctx, (sb + 1) * n_ctx)
            k_all, v_all = ctx_refs[2 * sb], ctx_refs[2 * sb + 1]
            for kvh in range(N_KV_HEADS):
                hd = slice(kvh * HEAD_DIM, (kvh + 1) * HEAD_DIM)
                qs = _stack_heads(q_ref[rows, kvh * GQA_GROUP * HEAD_DIM:(kvh + 1) * GQA_GROUP * HEAD_DIM])
                o = _attend(qs, [(k_all[:, hd], v_all[:, hd], None)], _sink_col(sink_ref, kvh, n_ctx))
                for g in range(GQA_GROUP):
                    c0 = (kvh * GQA_GROUP + g) * HEAD_DIM
                    attn_scr[rows, c0:c0 + HEAD_DIM] = o[g * n_ctx:(g + 1) * n_ctx]

    attn_n = (_rms(attn_scr[...]) * goa_ref[...]).astype(BF16)

    y = _dot(conv_n, wout_ref[:D_CONV, :]) + _dot(attn_n, wout_ref[D_CONV:, :])
    o_ref[...] = x_ref[...] + m_ref[0, 2:3, :] * y


def _mixer(x, p, w_out, conv_w, conv_b, g_oc, g_oa, sink, mods, geo, n_ctx, n_out_tiles):
    n_tiles, tpb, n_lat_tiles, n_groups = geo
    n_seq_ctx = TM // n_ctx
    halo_blocks = TM // BF16_SUBLANES
    kv_blocks = TM // BLOCK
    n_halo, n_kvb = n_tiles * halo_blocks, n_tiles * kv_blocks
    col_cg, col_hh, col_q = 1, 2, 3
    col_k, col_v = KV_START // D_KV, KV_START // D_KV + 1
    ctx_blk0 = n_lat_tiles * TM // n_ctx
    seq_of = lambda i: jnp.minimum(i // tpb, n_seq_ctx - 1)
    group = lambda i: (jnp.minimum(i // tpb, n_groups - 1), 0, 0)

    def halo(col, nxt):
        if nxt:
            return pl.BlockSpec((BF16_SUBLANES, D_CONV),
                                lambda i: (jnp.minimum((i + 1) * halo_blocks, n_halo - 1), col))
        return pl.BlockSpec((BF16_SUBLANES, D_CONV), lambda i: (jnp.maximum(i * halo_blocks - 1, 0), col))

    def kv(col, where):
        if where == 0:
            return pl.BlockSpec((TM, D_KV), lambda i: (i, col))
        if where < 0:
            return pl.BlockSpec((BLOCK, D_KV), lambda i: (jnp.maximum(i * kv_blocks - 1, 0), col))
        return pl.BlockSpec((BLOCK, D_KV), lambda i: (jnp.minimum((i + 1) * kv_blocks, n_kvb - 1), col))

    in_specs = [
        pl.BlockSpec((TM, D_MODEL), lambda i: (i, 0)),
        pl.BlockSpec((TM, D_CONV), lambda i: (i, 0)),
        pl.BlockSpec((TM, D_CONV), lambda i: (i, col_cg)),
        pl.BlockSpec((TM, D_CONV), lambda i: (i, col_hh)),
        halo(col_cg, False), halo(col_hh, False), halo(col_cg, True), halo(col_hh, True),
        pl.BlockSpec((TM, D_ATTN), lambda i: (i, col_q)),
        kv(col_k, -1), kv(col_k, 0), kv(col_k, 1),
        kv(col_v, -1), kv(col_v, 0), kv(col_v, 1),
        pl.BlockSpec((n_ctx, D_KV), lambda i: (ctx_blk0 + seq_of(i), col_k)),
        pl.BlockSpec((n_ctx, D_KV), lambda i: (ctx_blk0 + seq_of(i), col_v)),
    ]
    args = [x, p, p, p, p, p, p, p, p, p, p, p, p, p, p, p, p]
    for sb in range(n_seq_ctx):
        in_specs += [pl.BlockSpec((n_ctx, D_KV), functools.partial(lambda i, s: (ctx_blk0 + s, col_k), s=sb)),
                     pl.BlockSpec((n_ctx, D_KV), functools.partial(lambda i, s: (ctx_blk0 + s, col_v), s=sb))]
        args += [p, p]
    in_specs += [
        pl.BlockSpec((D_MODEL, D_MODEL), lambda i: (0, 0), pipeline_mode=pl.Buffered(1)),
        pl.BlockSpec((3, D_CONV), lambda i: (0, 0)),
        pl.BlockSpec((1, D_CONV), lambda i: (0, 0)),
        pl.BlockSpec((1, D_CONV), lambda i: (0, 0)),
        pl.BlockSpec((1, D_ATTN), lambda i: (0, 0)),
        pl.BlockSpec(memory_space=pltpu.SMEM),
        pl.BlockSpec((1, N_MOD, D_MODEL), group),
    ]
    args += [w_out, conv_w, conv_b, g_oc, g_oa, sink, mods]
    kern = functools.partial(_mixer_kernel, n_lat_tiles=n_lat_tiles, tpb=tpb, n_ctx=n_ctx, n_seq_ctx=n_seq_ctx)
    return pl.pallas_call(
        kern,
        grid=(n_out_tiles,),
        in_specs=in_specs,
        out_specs=pl.BlockSpec((TM, D_MODEL), lambda i: (i, 0)),
        out_shape=jax.ShapeDtypeStruct((n_out_tiles * TM, D_MODEL), F32),
        scratch_shapes=[pltpu.VMEM((TM, D_ATTN), F32)],
        compiler_params=_params(("arbitrary",)),
        name="mixer",
    )(*args)


def _mlp_kernel(xc_ref, xn_ref, g_ref, mc_ref, mn_ref, w1_ref, w2_ref, gf_ref, o_ref, h_even, h_odd, *, final):
    i, f = pl.program_id(0), pl.program_id(1)
    n_f = D_FF // TF
    rows_per = TM // n_f

    @pl.when((i == 0) & (f == 0))
    def _():
        h_even[...] = _norm_mod(xc_ref[...], g_ref[...], mc_ref[0], 3, 4)

    @pl.when(f == 0)
    def _():
        o_ref[...] = xc_ref[...]

    for parity, (h_cur, h_nxt) in enumerate(((h_even, h_odd), (h_odd, h_even))):
        @pl.when(i % 2 == parity)
        def _():
            r0 = pl.multiple_of(f * rows_per, rows_per)
            h_nxt[pl.ds(r0, rows_per), :] = _norm_mod(xn_ref[pl.ds(r0, rows_per), :], g_ref[...], mn_ref[0], 3, 4)
            hid = jnp.square(jnp.maximum(_dot(h_cur[...], w1_ref[...]), 0.0)).astype(BF16)
            o_ref[...] += mc_ref[0, 5:6, :] * _dot(hid, w2_ref[...])

    if final:
        @pl.when(f == n_f - 1)
        def _():
            o_ref[...] = _rms(o_ref[...]) * gf_ref[...]


def _mlp(x, g, mods, w1, w2, g_final, geo, n_out_tiles, final):
    n_tiles, tpb, n_lat_tiles, n_groups = geo
    nxt = lambda i: jnp.minimum(i + 1, n_out_tiles - 1)
    group = lambda i, f: (jnp.minimum(i // tpb, n_groups - 1), 0, 0)
    group_nxt = lambda i, f: (jnp.minimum(nxt(i) // tpb, n_groups - 1), 0, 0)
    return pl.pallas_call(
        functools.partial(_mlp_kernel, final=final),
        grid=(n_out_tiles, D_FF // TF),
        in_specs=[
            pl.BlockSpec((TM, D_MODEL), lambda i, f: (i, 0)),
            pl.BlockSpec((TM, D_MODEL), lambda i, f: (nxt(i), 0)),
            pl.BlockSpec((1, D_MODEL), lambda i, f: (0, 0)),
            pl.BlockSpec((1, N_MOD, D_MODEL), group),
            pl.BlockSpec((1, N_MOD, D_MODEL), group_nxt),
            pl.BlockSpec((D_MODEL, TF), lambda i, f: (0, f)),
            pl.BlockSpec((TF, D_MODEL), lambda i, f: (f, 0)),
            pl.BlockSpec((1, D_MODEL), lambda i, f: (0, 0)),
        ],
        out_specs=pl.BlockSpec((TM, D_MODEL), lambda i, f: (i, 0)),
        out_shape=jax.ShapeDtypeStruct((n_out_tiles * TM, D_MODEL), F32),
        scratch_shapes=[pltpu.VMEM((TM, D_MODEL), BF16), pltpu.VMEM((TM, D_MODEL), BF16)],
        compiler_params=_params(("arbitrary", "arbitrary")),
        name="mlp",
    )(x, x, g, mods, mods, w1, w2, g_final)


def _rope_tables(n_tok):
    rows = n_tok // GRID_W
    row_pos = jnp.repeat(jnp.arange(rows, dtype=F32), GRID_W)
    col_pos = jnp.tile(jnp.arange(GRID_W, dtype=F32), rows)
    inv = ROPE_THETA ** (-jnp.arange(0, ROPE_AXIS_DIM, 2, dtype=F32) / ROPE_AXIS_DIM)
    ang_r = row_pos[:, None] * inv[None, :]
    ang_c = col_pos[:, None] * inv[None, :]
    zeros = jnp.zeros_like(ang_r)
    cos_h = jnp.concatenate([jnp.cos(ang_r)] * 2 + [jnp.cos(ang_c)] * 2, axis=-1)
    hi_h = jnp.concatenate([zeros, jnp.sin(ang_r), zeros, jnp.sin(ang_c)], axis=-1)
    lo_h = jnp.concatenate([-jnp.sin(ang_r), zeros, -jnp.sin(ang_c), zeros], axis=-1)
    reps = LANES // HEAD_DIM
    pad = lambda t, fill: jnp.concatenate([jnp.tile(t, (1, reps)), jnp.full((TM, LANES), fill, F32)], axis=0)
    return pad(cos_h, 1.0), pad(hi_h, 0.0), pad(lo_h, 0.0)


def kernel(x, c, ctx, c_ctx, w_ada, b_ada, g_norm1, g_norm2, w_in, conv_w, conv_b, sink,
           g_out_conv, g_out_attn, w_out, w_mlp1, w_mlp2, g_final):
    bsz, n_tok, d = x.shape
    n_ctx = ctx.shape[1]
    depth = w_ada.shape[0]
    assert d == D_MODEL and n_tok % TM == 0 and n_tok % GRID_W == 0
    assert bsz * n_ctx == TM and n_ctx % BLOCK == 0 and bsz + 1 <= MOD_ROWS
    tpb = n_tok // TM
    n_lat_tiles = bsz * tpb
    n_tiles = n_lat_tiles + 1
    geo = (n_tiles, tpb, n_lat_tiles, bsz + 1)

    cc = jnp.concatenate([c, c_ctx[None, :], jnp.zeros((MOD_ROWS - bsz - 1, d), F32)], axis=0)
    mods = _ada(cc, w_ada, b_ada).reshape(depth, MOD_ROWS, N_MOD, d)
    tabs = _rope_tables(n_tok)
    xs = jnp.concatenate([x.reshape(bsz * n_tok, d), ctx.reshape(bsz * n_ctx, d)], axis=0)
    row = lambda v: v.reshape(1, -1)

    for i in range(depth):
        last = i == depth - 1
        n_out = n_lat_tiles if last else n_tiles
        p = _inproj(xs, row(g_norm1[i]), mods[i], w_in[i].astype(BF16), tabs, geo)
        xs = _mixer(xs, p, w_out[i].astype(BF16), conv_w[i], row(conv_b[i]), row(g_out_conv[i]),
                    row(g_out_attn[i]), sink[i], mods[i], geo, n_ctx, n_out)
        xs = _mlp(xs, row(g_norm2[i]), mods[i], w_mlp1[i].astype(BF16), w_mlp2[i].astype(BF16),
                  row(g_final), geo, n_out, last)
    return xs.reshape(bsz, n_tok, d)
```

```python
import functools

import jax
import jax.numpy as jnp
from jax import lax
from jax.experimental import pallas as pl
from jax.experimental.pallas import tpu as pltpu

D_MODEL = 2048
D_CONV = D_MODEL // 2
D_ATTN = D_MODEL - D_CONV
HEAD_DIM = 64
N_HEADS = D_ATTN // HEAD_DIM
N_KV_HEADS = 4
GQA_GROUP = N_HEADS // N_KV_HEADS
D_KV = N_KV_HEADS * HEAD_DIM
D_QG = GQA_GROUP * HEAD_DIM
GRID_W = 64
WINDOW = 128
BLOCK = 128
D_FF = 4 * D_MODEL
ROPE_THETA = 10000.0
ROPE_AXIS_DIM = HEAD_DIM // 2
ROPE_HALF = ROPE_AXIS_DIM // 2
EPS = 1e-6
N_MOD = 6
KV_START = 3 * D_CONV + D_ATTN
D_IN_PROJ = KV_START + 2 * D_KV
SCALE = HEAD_DIM ** -0.5
NEG_INF = -1e30

LANES = 128
BF16_SUBLANES = 16
MOD_ROWS = 8
TM = 512
TN_IN = 512
TF = 1024
TN_ADA = 1024
D_KDUP = N_KV_HEADS * LANES
N_QB = TM // BLOCK
VMEM_LIMIT = 56 * 1024 * 1024

F32 = jnp.float32
BF16 = jnp.bfloat16


def _params(sem):
    return pltpu.CompilerParams(dimension_semantics=sem, vmem_limit_bytes=VMEM_LIMIT)


def _rms(v):
    return v * lax.rsqrt(jnp.mean(v * v, axis=-1, keepdims=True) + EPS)


def _dot(a, b):
    return jnp.dot(a, b, preferred_element_type=F32)


def _dot_nt(a, b):
    return lax.dot_general(a, b, (((1,), (1,)), ((), ())), preferred_element_type=F32)


def _norm_mod(x, g, m, shift_row, scale_row):
    y = _rms(x) * g
    return (y * (1 + m[scale_row:scale_row + 1, :]) + m[shift_row:shift_row + 1, :]).astype(BF16)


def _ada_kernel(c_ref, w_ref, b_ref, o_ref):
    c = c_ref[...]
    s = (c * jax.nn.sigmoid(c)).astype(BF16)
    o_ref[0] = _dot(s, w_ref[0].astype(BF16)) + b_ref[0]


def _ada(cc, w_ada, b_ada):
    depth = w_ada.shape[0]
    n_out = N_MOD * D_MODEL
    return pl.pallas_call(
        _ada_kernel,
        grid=(depth, n_out // TN_ADA),
        in_specs=[
            pl.BlockSpec((MOD_ROWS, D_MODEL), lambda l, j: (0, 0)),
            pl.BlockSpec((1, D_MODEL, TN_ADA), lambda l, j: (l, 0, j)),
            pl.BlockSpec((1, 1, TN_ADA), lambda l, j: (l, 0, j)),
        ],
        out_specs=pl.BlockSpec((1, MOD_ROWS, TN_ADA), lambda l, j: (l, 0, j)),
        out_shape=jax.ShapeDtypeStruct((depth, MOD_ROWS, n_out), F32),
        compiler_params=_params(("arbitrary", "arbitrary")),
        name="adaln",
    )(cc, w_ada, b_ada.reshape(depth, 1, n_out))


def _rope(a, cos, sin_hi, sin_lo):
    outs = []
    for c in range(a.shape[1] // LANES):
        v = a[:, c * LANES:(c + 1) * LANES]
        outs.append(v * cos + pltpu.roll(v, ROPE_HALF, 1) * sin_hi
                    + pltpu.roll(v, LANES - ROPE_HALF, 1) * sin_lo)
    return outs


def _inproj_body(h_ref, w_ref, cos_ref, shi_ref, slo_ref, p_ref, kd_ref, vt_ref):
    h = h_ref[...]
    cos, shi, slo = cos_ref[...], shi_ref[...], slo_ref[...]
    q_first = (3 * D_CONV) // TN_IN
    kv_tile = KV_START // TN_IN
    low_half = lax.broadcasted_iota(jnp.int32, (TM, LANES), 1) < HEAD_DIM
    for c in range(D_IN_PROJ // TN_IN):
        cols = slice(c * TN_IN, (c + 1) * TN_IN)
        acc = _dot(h, w_ref[:, cols])
        if c < q_first:
            p_ref[:, cols] = acc.astype(BF16)
        elif c < kv_tile:
            for k, v in enumerate(_rope(acc, cos, shi, slo)):
                p_ref[:, c * TN_IN + k * LANES:c * TN_IN + (k + 1) * LANES] = (v * SCALE).astype(BF16)
        else:
            for k, v in enumerate(_rope(acc[:, :D_KV], cos, shi, slo)):
                swapped = pltpu.roll(v, HEAD_DIM, 1)
                kd_ref[:, (2 * k) * LANES:(2 * k + 1) * LANES] = jnp.where(low_half, v, swapped).astype(BF16)
                kd_ref[:, (2 * k + 1) * LANES:(2 * k + 2) * LANES] = jnp.where(low_half, swapped, v).astype(BF16)
            vt_ref[...] = acc[:, D_KV:].T.astype(BF16)


def _inproj_kernel(x0_ref, xn_ref, g_ref, m0_ref, mn_ref, w_ref, cos_ref, shi_ref, slo_ref,
                   p_ref, kd_ref, vt_ref, h_even, h_odd):
    i = pl.program_id(0)

    @pl.when(i == 0)
    def _():
        h_even[...] = _norm_mod(x0_ref[...], g_ref[...], m0_ref[0], 0, 1)

    for parity, (h_cur, h_nxt) in enumerate(((h_even, h_odd), (h_odd, h_even))):
        @pl.when(i % 2 == parity)
        def _():
            h_nxt[...] = _norm_mod(xn_ref[...], g_ref[...], mn_ref[0], 0, 1)
            _inproj_body(h_cur, w_ref, cos_ref, shi_ref, slo_ref, p_ref, kd_ref, vt_ref)


def _inproj(x, g, mods, w, tabs, geo):
    n_tiles, tpb, n_lat_tiles, n_groups = geo
    nxt = lambda i: jnp.minimum(i + 1, n_tiles - 1)
    group_nxt = lambda i: (jnp.minimum(nxt(i) // tpb, n_groups - 1), 0, 0)
    tab_idx = lambda i: (jnp.where(i < n_lat_tiles, i % tpb, tpb), 0)
    once = pl.Buffered(1)
    rows = n_tiles * TM
    return pl.pallas_call(
        _inproj_kernel,
        grid=(n_tiles,),
        in_specs=[
            pl.BlockSpec((TM, D_MODEL), lambda i: (0, 0), pipeline_mode=once),
            pl.BlockSpec((TM, D_MODEL), lambda i: (nxt(i), 0)),
            pl.BlockSpec((1, D_MODEL), lambda i: (0, 0)),
            pl.BlockSpec((1, N_MOD, D_MODEL), lambda i: (0, 0, 0)),
            pl.BlockSpec((1, N_MOD, D_MODEL), group_nxt),
            pl.BlockSpec((D_MODEL, D_IN_PROJ), lambda i: (0, 0), pipeline_mode=once),
            pl.BlockSpec((TM, LANES), tab_idx),
            pl.BlockSpec((TM, LANES), tab_idx),
            pl.BlockSpec((TM, LANES), tab_idx),
        ],
        out_specs=[
            pl.BlockSpec((TM, KV_START), lambda i: (i, 0)),
            pl.BlockSpec((TM, D_KDUP), lambda i: (i, 0)),
            pl.BlockSpec((D_KV, TM), lambda i: (0, i)),
        ],
        out_shape=[
            jax.ShapeDtypeStruct((rows, KV_START), BF16),
            jax.ShapeDtypeStruct((rows, D_KDUP), BF16),
            jax.ShapeDtypeStruct((D_KV, rows), BF16),
        ],
        scratch_shapes=[pltpu.VMEM((TM, D_MODEL), BF16), pltpu.VMEM((TM, D_MODEL), BF16)],
        compiler_params=_params(("arbitrary",)),
        name="inproj",
    )(x, x, g, mods, mods, w, *tabs)


def _band_bias(first, last):
    jj = lax.broadcasted_iota(jnp.int32, (3 * BLOCK, GQA_GROUP * BLOCK), 0)
    rr = lax.broadcasted_iota(jnp.int32, (3 * BLOCK, GQA_GROUP * BLOCK), 1) % BLOCK
    ok = (jj >= rr) & (jj <= rr + 2 * WINDOW)
    if first:
        ok = ok & (jj >= BLOCK)
    if last:
        ok = ok & (jj < 2 * BLOCK)
    return jnp.where(ok, 0.0, NEG_INF).astype(F32)


def _attend_t(q_blk, head_masks, parts, snk):
    q_bd = jnp.concatenate([q_blk * hm for hm in head_masks], axis=0)
    scores = []
    for k4, _, bias in parts:
        s = _dot_nt(k4, q_bd)
        scores.append(s if bias is None else s + bias)
    m = snk
    for s in scores:
        m = jnp.maximum(m, jnp.max(s, axis=0, keepdims=True))
    denom = jnp.exp(snk - m)
    o = None
    for s, (_, vt, _) in zip(scores, parts):
        e = jnp.exp(s - m)
        denom = denom + jnp.sum(e, axis=0, keepdims=True)
        pv = _dot(vt, e.astype(BF16))
        o = pv if o is None else o + pv
    return o / denom


def _mixer_kernel(x_ref, bg_ref, cg_ref, hh_ref, cgp_ref, hhp_ref, cgn_ref, hhn_ref, q_ref,
                  kp_ref, kc_ref, kn_ref, vp_ref, vc_ref, vn_ref, ck_ref, cv_ref, *rest,
                  n_lat_tiles, tpb, n_ctx, n_seq_ctx):
    ctx_refs = rest[:2 * n_seq_ctx]
    (wout_ref, cw_ref, cb_ref, goc_ref, goa_ref, sink_ref, m_ref, o_ref, attn_scr, bias_scr) = rest[2 * n_seq_ctx:]
    i = pl.program_id(0)
    is_lat = i < n_lat_tiles
    tib = i % tpb

    @pl.when(i == 0)
    def _():
        bias_scr[0] = _band_bias(True, False)
        bias_scr[1] = _band_bias(False, False)
        bias_scr[2] = _band_bias(False, True)

    u = cg_ref[...].astype(F32) * hh_ref[...].astype(F32)
    last = BF16_SUBLANES - 1
    prev_edge = cgp_ref[last:last + 1, :].astype(F32) * hhp_ref[last:last + 1, :].astype(F32)
    next_edge = cgn_ref[0:1, :].astype(F32) * hhn_ref[0:1, :].astype(F32)
    prev_edge = jnp.where(is_lat & (tib > 0), prev_edge, 0.0)
    next_edge = jnp.where(is_lat & (tib < tpb - 1), next_edge, 0.0)
    r = lax.broadcasted_iota(jnp.int32, (TM, 1), 0)
    not_lat = jnp.logical_not(is_lat)
    seq_start, seq_end = r == 0, r == TM - 1
    for sb in range(1, n_seq_ctx):
        edge = jnp.where(is_lat, -1, sb * n_ctx)
        seq_start = seq_start | (r == edge)
        seq_end = seq_end | (r == edge - 1)
    u_prev = jnp.where(seq_start, prev_edge, pltpu.roll(u, 1, 0))
    u_next = jnp.where(seq_end, next_edge, pltpu.roll(u, TM - 1, 0))
    conv = u_prev * cw_ref[0:1, :] + u * cw_ref[1:2, :] + u_next * cw_ref[2:3, :] + cb_ref[...]
    conv_n = (_rms(bg_ref[...].astype(F32) * conv) * goc_ref[...]).astype(BF16)

    lane_head = lax.broadcasted_iota(jnp.int32, (1, D_QG), 1) // HEAD_DIM
    head_masks = [(lane_head == g).astype(F32).astype(BF16) for g in range(GQA_GROUP)]

    def sink_row(kvh):
        return jnp.concatenate(
            [jnp.full((1, BLOCK), sink_ref[kvh * GQA_GROUP + g], F32) for g in range(GQA_GROUP)], axis=1)

    def keys4(kd, kvh):
        k2 = kd[:, kvh * LANES:(kvh + 1) * LANES]
        return jnp.concatenate([k2] * (D_QG // LANES), axis=1)

    def put(kvh, qb, o_t):
        for g in range(GQA_GROUP):
            r0 = (kvh * GQA_GROUP + g) * HEAD_DIM
            attn_scr[r0:r0 + HEAD_DIM, qb * BLOCK:(qb + 1) * BLOCK] = o_t[:, g * BLOCK:(g + 1) * BLOCK]

    @pl.when(is_lat)
    def _():
        kd_ext = jnp.concatenate([kp_ref[...], kc_ref[...], kn_ref[...]], axis=0)
        vt_ext = jnp.concatenate([vp_ref[...], vc_ref[...], vn_ref[...]], axis=1)
        first_idx = jnp.where(tib == 0, 0, 1)
        last_idx = jnp.where(tib == tpb - 1, 2, 1)
        for kvh in range(N_KV_HEADS):
            hd = slice(kvh * HEAD_DIM, (kvh + 1) * HEAD_DIM)
            k4_ext = keys4(kd_ext, kvh)
            k4_ctx = keys4(ck_ref[...], kvh)
            vt_ctx = cv_ref[hd, :]
            snk = sink_row(kvh)
            for qb in range(N_QB):
                bias = bias_scr[first_idx if qb == 0 else last_idx if qb == N_QB - 1 else 1]
                win = slice(qb * BLOCK, qb * BLOCK + 3 * BLOCK)
                o_t = _attend_t(q_ref[qb * BLOCK:(qb + 1) * BLOCK, kvh * D_QG:(kvh + 1) * D_QG], head_masks,
                                [(k4_ext[win], vt_ext[hd, win], bias), (k4_ctx, vt_ctx, None)], snk)
                put(kvh, qb, o_t)

    @pl.when(not_lat)
    def _():
        for sb in range(n_seq_ctx):
            k_all, vt_all = ctx_refs[2 * sb], ctx_refs[2 * sb + 1]
            for kvh in range(N_KV_HEADS):
                hd = slice(kvh * HEAD_DIM, (kvh + 1) * HEAD_DIM)
                k4 = keys4(k_all[...], kvh)
                snk = sink_row(kvh)
                for qb in range(sb * n_ctx // BLOCK, (sb + 1) * n_ctx // BLOCK):
                    o_t = _attend_t(q_ref[qb * BLOCK:(qb + 1) * BLOCK, kvh * D_QG:(kvh + 1) * D_QG], head_masks,
                                    [(k4, vt_all[hd, :], None)], snk)
                    put(kvh, qb, o_t)

    a_t = attn_scr[...]
    rstd = lax.rsqrt(jnp.mean(a_t * a_t, axis=0, keepdims=True) + EPS)
    goa_t = jnp.concatenate([goa_ref[...]] * (TM // LANES), axis=1)
    attn_n = (a_t * rstd * goa_t).T.astype(BF16)

    y = _dot(conv_n, wout_ref[:D_CONV, :]) + _dot(attn_n, wout_ref[D_CONV:, :])
    o_ref[...] = x_ref[...] + m_ref[0, 2:3, :] * y


def _mixer(x, p, kd, vt, w_out, conv_w, conv_b, g_oc, g_oa_t, sink, mods, geo, n_ctx, n_out_tiles):
    n_tiles, tpb, n_lat_tiles, n_groups = geo
    n_seq_ctx = TM // n_ctx
    halo_blocks = TM // BF16_SUBLANES
    n_halo, n_kvb = n_tiles * halo_blocks, n_tiles * N_QB
    col_cg, col_hh, col_q = 1, 2, 3
    ctx_blk0 = n_lat_tiles * TM // n_ctx
    seq_of = lambda i: jnp.minimum(i // tpb, n_seq_ctx - 1)
    group = lambda i: (jnp.minimum(i // tpb, n_groups - 1), 0, 0)
    prev_blk = lambda i: jnp.maximum(i * N_QB - 1, 0)
    next_blk = lambda i: jnp.minimum((i + 1) * N_QB, n_kvb - 1)

    def halo(col, nxt):
        if nxt:
            return pl.BlockSpec((BF16_SUBLANES, D_CONV),
                                lambda i: (jnp.minimum((i + 1) * halo_blocks, n_halo - 1), col))
        return pl.BlockSpec((BF16_SUBLANES, D_CONV), lambda i: (jnp.maximum(i * halo_blocks - 1, 0), col))

    in_specs = [
        pl.BlockSpec((TM, D_MODEL), lambda i: (i, 0)),
        pl.BlockSpec((TM, D_CONV), lambda i: (i, 0)),
        pl.BlockSpec((TM, D_CONV), lambda i: (i, col_cg)),
        pl.BlockSpec((TM, D_CONV), lambda i: (i, col_hh)),
        halo(col_cg, False), halo(col_hh, False), halo(col_cg, True), halo(col_hh, True),
        pl.BlockSpec((TM, D_ATTN), lambda i: (i, col_q)),
        pl.BlockSpec((BLOCK, D_KDUP), lambda i: (prev_blk(i), 0)),
        pl.BlockSpec((TM, D_KDUP), lambda i: (i, 0)),
        pl.BlockSpec((BLOCK, D_KDUP), lambda i: (next_blk(i), 0)),
        pl.BlockSpec((D_KV, BLOCK), lambda i: (0, prev_blk(i))),
        pl.BlockSpec((D_KV, TM), lambda i: (0, i)),
        pl.BlockSpec((D_KV, BLOCK), lambda i: (0, next_blk(i))),
        pl.BlockSpec((n_ctx, D_KDUP), lambda i: (ctx_blk0 + seq_of(i), 0)),
        pl.BlockSpec((D_KV, n_ctx), lambda i: (0, ctx_blk0 + seq_of(i))),
    ]
    args = [x, p, p, p, p, p, p, p, p, kd, kd, kd, vt, vt, vt, kd, vt]
    for sb in range(n_seq_ctx):
        in_specs += [pl.BlockSpec((n_ctx, D_KDUP), functools.partial(lambda i, s: (ctx_blk0 + s, 0), s=sb)),
                     pl.BlockSpec((D_KV, n_ctx), functools.partial(lambda i, s: (0, ctx_blk0 + s), s=sb))]
        args += [kd, vt]
    in_specs += [
        pl.BlockSpec((D_MODEL, D_MODEL), lambda i: (0, 0), pipeline_mode=pl.Buffered(1)),
        pl.BlockSpec((3, D_CONV), lambda i: (0, 0)),
        pl.BlockSpec((1, D_CONV), lambda i: (0, 0)),
        pl.BlockSpec((1, D_CONV), lambda i: (0, 0)),
        pl.BlockSpec((D_ATTN, LANES), lambda i: (0, 0)),
        pl.BlockSpec(memory_space=pltpu.SMEM),
        pl.BlockSpec((1, N_MOD, D_MODEL), group),
    ]
    args += [w_out, conv_w, conv_b, g_oc, g_oa_t, sink, mods]
    kern = functools.partial(_mixer_kernel, n_lat_tiles=n_lat_tiles, tpb=tpb, n_ctx=n_ctx, n_seq_ctx=n_seq_ctx)
    return pl.pallas_call(
        kern,
        grid=(n_out_tiles,),
        in_specs=in_specs,
        out_specs=pl.BlockSpec((TM, D_MODEL), lambda i: (i, 0)),
        out_shape=jax.ShapeDtypeStruct((n_out_tiles * TM, D_MODEL), F32),
        scratch_shapes=[pltpu.VMEM((D_ATTN, TM), F32),
                        pltpu.VMEM((3, 3 * BLOCK, GQA_GROUP * BLOCK), F32)],
        compiler_params=_params(("arbitrary",)),
        name="mixer",
    )(*args)


def _mlp_kernel(xc_ref, xn_ref, g_ref, mc_ref, mn_ref, w1_ref, w2_ref, gf_ref, o_ref, h_even, h_odd, *, final):
    i, f = pl.program_id(0), pl.program_id(1)
    n_f = D_FF // TF
    rows_per = TM // n_f

    @pl.when((i == 0) & (f == 0))
    def _():
        h_even[...] = _norm_mod(xc_ref[...], g_ref[...], mc_ref[0], 3, 4)

    @pl.when(f == 0)
    def _():
        o_ref[...] = xc_ref[...]

    for parity, (h_cur, h_nxt) in enumerate(((h_even, h_odd), (h_odd, h_even))):
        @pl.when(i % 2 == parity)
        def _():
            r0 = pl.multiple_of(f * rows_per, rows_per)
            h_nxt[pl.ds(r0, rows_per), :] = _norm_mod(xn_ref[pl.ds(r0, rows_per), :], g_ref[...], mn_ref[0], 3, 4)
            hid = jnp.square(jnp.maximum(_dot(h_cur[...], w1_ref[...]), 0.0)).astype(BF16)
            o_ref[...] += mc_ref[0, 5:6, :] * _dot(hid, w2_ref[...])

    if final:
        @pl.when(f == n_f - 1)
        def _():
            o_ref[...] = _rms(o_ref[...]) * gf_ref[...]


def _mlp(x, g, mods, w1, w2, g_final, geo, n_out_tiles, final):
    n_tiles, tpb, n_lat_tiles, n_groups = geo
    nxt = lambda i: jnp.minimum(i + 1, n_out_tiles - 1)
    group = lambda i, f: (jnp.minimum(i // tpb, n_groups - 1), 0, 0)
    group_nxt = lambda i, f: (jnp.minimum(nxt(i) // tpb, n_groups - 1), 0, 0)
    return pl.pallas_call(
        functools.partial(_mlp_kernel, final=final),
        grid=(n_out_tiles, D_FF // TF),
        in_specs=[
            pl.BlockSpec((TM, D_MODEL), lambda i, f: (i, 0)),
            pl.BlockSpec((TM, D_MODEL), lambda i, f: (nxt(i), 0)),
            pl.BlockSpec((1, D_MODEL), lambda i, f: (0, 0)),
            pl.BlockSpec((1, N_MOD, D_MODEL), group),
            pl.BlockSpec((1, N_MOD, D_MODEL), group_nxt),
            pl.BlockSpec((D_MODEL, TF), lambda i, f: (0, f)),
            pl.BlockSpec((TF, D_MODEL), lambda i, f: (f, 0)),
            pl.BlockSpec((1, D_MODEL), lambda i, f: (0, 0)),
        ],
        out_specs=pl.BlockSpec((TM, D_MODEL), lambda i, f: (i, 0)),
        out_shape=jax.ShapeDtypeStruct((n_out_tiles * TM, D_MODEL), F32),
        scratch_shapes=[pltpu.VMEM((TM, D_MODEL), BF16), pltpu.VMEM((TM, D_MODEL), BF16)],
        compiler_params=_params(("arbitrary", "arbitrary")),
        name="mlp",
    )(x, x, g, mods, mods, w1, w2, g_final)


def _rope_tables(n_tok):
    rows = n_tok // GRID_W
    row_pos = jnp.repeat(jnp.arange(rows, dtype=F32), GRID_W)
    col_pos = jnp.tile(jnp.arange(GRID_W, dtype=F32), rows)
    inv = ROPE_THETA ** (-jnp.arange(0, ROPE_AXIS_DIM, 2, dtype=F32) / ROPE_AXIS_DIM)
    ang_r = row_pos[:, None] * inv[None, :]
    ang_c = col_pos[:, None] * inv[None, :]
    zeros = jnp.zeros_like(ang_r)
    cos_h = jnp.concatenate([jnp.cos(ang_r)] * 2 + [jnp.cos(ang_c)] * 2, axis=-1)
    hi_h = jnp.concatenate([zeros, jnp.sin(ang_r), zeros, jnp.sin(ang_c)], axis=-1)
    lo_h = jnp.concatenate([-jnp.sin(ang_r), zeros, -jnp.sin(ang_c), zeros], axis=-1)
    reps = LANES // HEAD_DIM
    pad = lambda t, fill: jnp.concatenate([jnp.tile(t, (1, reps)), jnp.full((TM, LANES), fill, F32)], axis=0)
    return pad(cos_h, 1.0), pad(hi_h, 0.0), pad(lo_h, 0.0)


def kernel(x, c, ctx, c_ctx, w_ada, b_ada, g_norm1, g_norm2, w_in, conv_w, conv_b, sink,
           g_out_conv, g_out_attn, w_out, w_mlp1, w_mlp2, g_final):
    bsz, n_tok, d = x.shape
    n_ctx = ctx.shape[1]
    depth = w_ada.shape[0]
    assert d == D_MODEL and n_tok % TM == 0 and n_tok % GRID_W == 0
    assert bsz * n_ctx == TM and n_ctx % BLOCK == 0 and bsz + 1 <= MOD_ROWS
    tpb = n_tok // TM
    n_lat_tiles = bsz * tpb
    n_tiles = n_lat_tiles + 1
    geo = (n_tiles, tpb, n_lat_tiles, bsz + 1)

    cc = jnp.concatenate([c, c_ctx[None, :], jnp.zeros((MOD_ROWS - bsz - 1, d), F32)], axis=0)
    mods = _ada(cc, w_ada, b_ada).reshape(depth, MOD_ROWS, N_MOD, d)
    tabs = _rope_tables(n_tok)
    xs = jnp.concatenate([x.reshape(bsz * n_tok, d), ctx.reshape(bsz * n_ctx, d)], axis=0)
    row = lambda v: v.reshape(1, -1)

    for i in range(depth):
        last = i == depth - 1
        n_out = n_lat_tiles if last else n_tiles
        p, kd, vt = _inproj(xs, row(g_norm1[i]), mods[i], w_in[i].astype(BF16), tabs, geo)
        g_oa_t = jnp.broadcast_to(g_out_attn[i][:, None], (D_ATTN, LANES))
        xs = _mixer(xs, p, kd, vt, w_out[i].astype(BF16), conv_w[i], row(conv_b[i]), row(g_out_conv[i]),
                    g_oa_t, sink[i], mods[i], geo, n_ctx, n_out)
        xs = _mlp(xs, row(g_norm2[i]), mods[i], w_mlp1[i].astype(BF16), w_mlp2[i].astype(BF16),
                  row(g_final), geo, n_out, last)
    return xs.reshape(bsz, n_tok, d)
```

```python
import functools

import jax
import jax.numpy as jnp
from jax import lax
from jax.experimental import pallas as pl
from jax.experimental.pallas import tpu as pltpu

D_MODEL = 2048
D_CONV = D_MODEL // 2
D_ATTN = D_MODEL - D_CONV
HEAD_DIM = 64
N_HEADS = D_ATTN // HEAD_DIM
N_KV_HEADS = 4
GQA_GROUP = N_HEADS // N_KV_HEADS
D_KV = N_KV_HEADS * HEAD_DIM
D_QG = GQA_GROUP * HEAD_DIM
GRID_W = 64
WINDOW = 128
BLOCK = 128
D_FF = 4 * D_MODEL
ROPE_THETA = 10000.0
ROPE_AXIS_DIM = HEAD_DIM // 2
ROPE_HALF = ROPE_AXIS_DIM // 2
EPS = 1e-6
N_MOD = 6
KV_START = 3 * D_CONV + D_ATTN
D_IN_PROJ = KV_START + 2 * D_KV
SCALE = HEAD_DIM ** -0.5
LOG2E = 1.4426950408889634
Q_SCALE = SCALE * LOG2E
NEG_INF = -1e30

LANES = 128
BF16_SUBLANES = 16
MOD_ROWS = 8
TM = 512
TN_IN = 512
TF = 1024
TN_ADA = 1024
D_KDUP = N_KV_HEADS * LANES
N_QB = TM // BLOCK
VMEM_LIMIT = 56 * 1024 * 1024

F32 = jnp.float32
BF16 = jnp.bfloat16


def _params(sem):
    return pltpu.CompilerParams(dimension_semantics=sem, vmem_limit_bytes=VMEM_LIMIT)


def _rms(v):
    return v * lax.rsqrt(jnp.mean(v * v, axis=-1, keepdims=True) + EPS)


def _dot(a, b):
    return jnp.dot(a, b, preferred_element_type=F32)


def _dot_nt(a, b):
    return lax.dot_general(a, b, (((1,), (1,)), ((), ())), preferred_element_type=F32)


def _norm_mod(x, g, m, shift_row, scale_row):
    y = _rms(x) * g
    return (y * (1 + m[scale_row:scale_row + 1, :]) + m[shift_row:shift_row + 1, :]).astype(BF16)


def _ada_kernel(c_ref, w_ref, b_ref, o_ref):
    c = c_ref[...]
    s = (c * jax.nn.sigmoid(c)).astype(BF16)
    o_ref[0] = _dot(s, w_ref[0].astype(BF16)) + b_ref[0]


def _ada(cc, w_ada, b_ada):
    depth = w_ada.shape[0]
    n_out = N_MOD * D_MODEL
    return pl.pallas_call(
        _ada_kernel,
        grid=(depth, n_out // TN_ADA),
        in_specs=[
            pl.BlockSpec((MOD_ROWS, D_MODEL), lambda l, j: (0, 0)),
            pl.BlockSpec((1, D_MODEL, TN_ADA), lambda l, j: (l, 0, j)),
            pl.BlockSpec((1, 1, TN_ADA), lambda l, j: (l, 0, j)),
        ],
        out_specs=pl.BlockSpec((1, MOD_ROWS, TN_ADA), lambda l, j: (l, 0, j)),
        out_shape=jax.ShapeDtypeStruct((depth, MOD_ROWS, n_out), F32),
        compiler_params=_params(("arbitrary", "arbitrary")),
        name="adaln",
    )(cc, w_ada, b_ada.reshape(depth, 1, n_out))


def _rope(a, cos, sin_hi, sin_lo):
    outs = []
    for c in range(a.shape[1] // LANES):
        v = a[:, c * LANES:(c + 1) * LANES]
        outs.append(v * cos + pltpu.roll(v, ROPE_HALF, 1) * sin_hi
                    + pltpu.roll(v, LANES - ROPE_HALF, 1) * sin_lo)
    return outs


def _inproj_body(h_ref, h_nxt, xn_ref, g_ref, mn_ref, w_ref, cos_ref, shi_ref, slo_ref, p_ref, kd_ref, vt_ref):
    h = h_ref[...]
    cos, shi, slo = cos_ref[...], shi_ref[...], slo_ref[...]
    n_chunks = D_IN_PROJ // TN_IN
    q_first = (3 * D_CONV) // TN_IN
    kv_tile = KV_START // TN_IN
    rows_per = TM // (n_chunks - 1)
    low_half = lax.broadcasted_iota(jnp.int32, (TM, LANES), 1) < HEAD_DIM
    order = [kv_tile] + list(range(q_first, kv_tile)) + list(range(q_first))
    for step, c in enumerate(order):
        cols = slice(c * TN_IN, (c + 1) * TN_IN)
        acc = _dot(h, w_ref[:, cols])
        if step < n_chunks - 1:
            rows = slice(step * rows_per, (step + 1) * rows_per)
            h_nxt[rows, :] = _norm_mod(xn_ref[rows, :], g_ref[...], mn_ref[0], 0, 1)
        if c < q_first:
            p_ref[:, cols] = acc.astype(BF16)
        elif c < kv_tile:
            for k, v in enumerate(_rope(acc, cos, shi, slo)):
                p_ref[:, c * TN_IN + k * LANES:c * TN_IN + (k + 1) * LANES] = (v * Q_SCALE).astype(BF16)
        else:
            for k, v in enumerate(_rope(acc[:, :D_KV], cos, shi, slo)):
                swapped = pltpu.roll(v, HEAD_DIM, 1)
                kd_ref[:, (2 * k) * LANES:(2 * k + 1) * LANES] = jnp.where(low_half, v, swapped).astype(BF16)
                kd_ref[:, (2 * k + 1) * LANES:(2 * k + 2) * LANES] = jnp.where(low_half, swapped, v).astype(BF16)
            vt_ref[...] = acc[:, D_KV:].T.astype(BF16)


def _inproj_kernel(x0_ref, xn_ref, g_ref, m0_ref, mn_ref, w_ref, cos_ref, shi_ref, slo_ref,
                   p_ref, kd_ref, vt_ref, h_even, h_odd):
    i = pl.program_id(0)

    @pl.when(i == 0)
    def _():
        h_even[...] = _norm_mod(x0_ref[...], g_ref[...], m0_ref[0], 0, 1)

    for parity, (h_cur, h_nxt) in enumerate(((h_even, h_odd), (h_odd, h_even))):
        @pl.when(i % 2 == parity)
        def _():
            _inproj_body(h_cur, h_nxt, xn_ref, g_ref, mn_ref, w_ref, cos_ref, shi_ref, slo_ref,
                         p_ref, kd_ref, vt_ref)


def _inproj(x, g, mods, w, layer, tabs, geo):
    n_tiles, tpb, n_lat_tiles, n_groups = geo
    nxt = lambda i: jnp.minimum(i + 1, n_tiles - 1)
    group_nxt = lambda i: (jnp.minimum(nxt(i) // tpb, n_groups - 1), 0, 0)
    tab_idx = lambda i: (jnp.where(i < n_lat_tiles, i % tpb, tpb), 0)
    once = pl.Buffered(1)
    rows = n_tiles * TM
    return pl.pallas_call(
        _inproj_kernel,
        grid=(n_tiles,),
        in_specs=[
            pl.BlockSpec((TM, D_MODEL), lambda i: (0, 0), pipeline_mode=once),
            pl.BlockSpec((TM, D_MODEL), lambda i: (nxt(i), 0)),
            pl.BlockSpec((1, D_MODEL), lambda i: (0, 0)),
            pl.BlockSpec((1, N_MOD, D_MODEL), lambda i: (0, 0, 0)),
            pl.BlockSpec((1, N_MOD, D_MODEL), group_nxt),
            pl.BlockSpec((None, D_MODEL, D_IN_PROJ), lambda i: (layer, 0, 0), pipeline_mode=once),
            pl.BlockSpec((TM, LANES), tab_idx),
            pl.BlockSpec((TM, LANES), tab_idx),
            pl.BlockSpec((TM, LANES), tab_idx),
        ],
        out_specs=[
            pl.BlockSpec((TM, KV_START), lambda i: (i, 0)),
            pl.BlockSpec((TM, D_KDUP), lambda i: (i, 0)),
            pl.BlockSpec((D_KV, TM), lambda i: (0, i)),
        ],
        out_shape=[
            jax.ShapeDtypeStruct((rows, KV_START), BF16),
            jax.ShapeDtypeStruct((rows, D_KDUP), BF16),
            jax.ShapeDtypeStruct((D_KV, rows), BF16),
        ],
        scratch_shapes=[pltpu.VMEM((TM, D_MODEL), BF16), pltpu.VMEM((TM, D_MODEL), BF16)],
        compiler_params=_params(("arbitrary",)),
        name="inproj",
    )(x, x, g, mods, mods, w, *tabs)


def _band_bias(first, last):
    jj = lax.broadcasted_iota(jnp.int32, (3 * BLOCK, GQA_GROUP * BLOCK), 0)
    rr = lax.broadcasted_iota(jnp.int32, (3 * BLOCK, GQA_GROUP * BLOCK), 1) % BLOCK
    ok = (jj >= rr) & (jj <= rr + 2 * WINDOW)
    if first:
        ok = ok & (jj >= BLOCK)
    if last:
        ok = ok & (jj < 2 * BLOCK)
    bias = jnp.where(ok, 0.0, NEG_INF).astype(F32)
    return bias[:BLOCK], bias[2 * BLOCK:]


def _attend_t(q_blk, head_masks, parts, snk):
    return _mix_values(*_scores(q_blk, head_masks, parts, snk), parts, snk)


def _scores(q_blk, head_masks, parts, snk):
    q_bd = jnp.concatenate([q_blk * hm for hm in head_masks], axis=0)
    s = _dot_nt(jnp.concatenate([k4 for k4, _, _ in parts], axis=0), q_bd)
    rows, r0 = [], 0
    for k4, _, bias in parts:
        n = k4.shape[0]
        if bias is None:
            rows.append(s[r0:r0 + n])
        else:
            rows += [s[r0:r0 + BLOCK] + bias[0], s[r0 + BLOCK:r0 + n - BLOCK], s[r0 + n - BLOCK:r0 + n] + bias[1]]
        r0 += n
    s = jnp.concatenate(rows, axis=0)
    return s, jnp.maximum(snk, jnp.max(s, axis=0, keepdims=True))


def _mix_values(s, m, parts, snk):
    acc = _dot(jnp.concatenate([vt for _, vt, _ in parts], axis=1), jnp.exp2(s - m).astype(BF16))
    denom = acc[HEAD_DIM:HEAD_DIM + 1, :] + jnp.exp2(snk - m)
    return acc[:HEAD_DIM, :] / denom


def _set_row(a, row, value):
    g0 = row // MOD_ROWS * MOD_ROWS
    r8 = lax.broadcasted_iota(jnp.int32, (MOD_ROWS, 1), 0)
    piece = jnp.where(r8 == row - g0, value, a[g0:g0 + MOD_ROWS])
    return jnp.concatenate([t for t in (a[:g0], piece, a[g0 + MOD_ROWS:]) if t.shape[0]], axis=0)


def _mixer_kernel(x_ref, bg_ref, cg_ref, hh_ref, cgp_ref, hhp_ref, cgn_ref, hhn_ref, q_ref,
                  kp_ref, kc_ref, kn_ref, vp_ref, vc_ref, vn_ref, ck_ref, cv_ref, *rest,
                  n_lat_tiles, tpb, n_ctx, n_seq_ctx):
    ctx_refs = rest[:2 * n_seq_ctx]
    (wout_ref, cw_ref, cb_ref, goc_ref, goa_ref, sink_ref, m_ref, o_ref, attn_scr, bias_scr) = rest[2 * n_seq_ctx:]
    i = pl.program_id(0)
    is_lat = i < n_lat_tiles
    tib = i % tpb

    @pl.when(i == 0)
    def _():
        for v, (first, last) in enumerate(((True, False), (False, False), (False, True))):
            lo, hi = _band_bias(first, last)
            bias_scr[v, 0] = lo
            bias_scr[v, 1] = hi

    lane_head = lax.broadcasted_iota(jnp.int32, (1, D_QG), 1) // HEAD_DIM
    head_masks = [(lane_head == g).astype(F32).astype(BF16) for g in range(GQA_GROUP)]
    ones_rows = jnp.ones((BF16_SUBLANES, TM + 2 * BLOCK), BF16)

    def conv_normed(lat):
        u = cg_ref[...].astype(F32) * hh_ref[...].astype(F32)
        u_prev, u_next = pltpu.roll(u, 1, 0), pltpu.roll(u, TM - 1, 0)
        if lat:
            last = BF16_SUBLANES - 1
            prev_edge = cgp_ref[last:last + 1, :].astype(F32) * hhp_ref[last:last + 1, :].astype(F32)
            next_edge = cgn_ref[0:1, :].astype(F32) * hhn_ref[0:1, :].astype(F32)
            u_prev = _set_row(u_prev, 0, jnp.where(tib > 0, prev_edge, 0.0))
            u_next = _set_row(u_next, TM - 1, jnp.where(tib < tpb - 1, next_edge, 0.0))
        else:
            for sb in range(n_seq_ctx):
                u_prev = _set_row(u_prev, sb * n_ctx, 0.0)
                u_next = _set_row(u_next, (sb + 1) * n_ctx - 1, 0.0)
        conv = u_prev * cw_ref[0:1, :] + u * cw_ref[1:2, :] + u_next * cw_ref[2:3, :] + cb_ref[...]
        return (_rms(bg_ref[...].astype(F32) * conv) * goc_ref[...]).astype(BF16)

    def sink_row(kvh):
        return jnp.concatenate(
            [jnp.full((1, BLOCK), sink_ref[kvh * GQA_GROUP + g] * LOG2E, F32) for g in range(GQA_GROUP)], axis=1)

    def keys4(kd, kvh):
        k2 = kd[:, kvh * LANES:(kvh + 1) * LANES]
        return jnp.concatenate([k2] * (D_QG // LANES), axis=1)

    def values_t(vt, kvh):
        return jnp.concatenate([vt[kvh * HEAD_DIM:(kvh + 1) * HEAD_DIM, :], ones_rows[:, :vt.shape[1]]], axis=0)

    def put(kvh, qb, o_t):
        for g in range(GQA_GROUP):
            r0 = (kvh * GQA_GROUP + g) * HEAD_DIM
            attn_scr[r0:r0 + HEAD_DIM, qb * BLOCK:(qb + 1) * BLOCK] = o_t[:, g * BLOCK:(g + 1) * BLOCK]

    def run_units(units, extras):
        extras = list(extras)
        every = max(1, len(units) // max(1, len(extras)))
        pending = None
        for u, unit in enumerate(units + [None]):
            issued = None
            if unit is not None:
                kvh, qb, parts, snk = unit
                q_blk = q_ref[qb * BLOCK:(qb + 1) * BLOCK, kvh * D_QG:(kvh + 1) * D_QG]
                issued = (kvh, qb, parts, snk, _scores(q_blk, head_masks, parts, snk))
            if pending is not None:
                kvh, qb, parts, snk, (s, m) = pending
                put(kvh, qb, _mix_values(s, m, parts, snk))
            if extras and u % every == 0:
                extras.pop(0)()
            pending = issued
        for extra in extras:
            extra()

    def attend_latent(extras):
        kd_ext = jnp.concatenate([kp_ref[...], kc_ref[...], kn_ref[...]], axis=0)
        vt_ext = jnp.concatenate([vp_ref[...], vc_ref[...], vn_ref[...]], axis=1)
        first_idx = jnp.where(tib == 0, 0, 1)
        last_idx = jnp.where(tib == tpb - 1, 2, 1)
        units = []
        for kvh in range(N_KV_HEADS):
            k4_ext, vt4_ext = keys4(kd_ext, kvh), values_t(vt_ext, kvh)
            k4_ctx, vt4_ctx = keys4(ck_ref[...], kvh), values_t(cv_ref[...], kvh)
            snk = sink_row(kvh)
            for qb in range(N_QB):
                v = first_idx if qb == 0 else last_idx if qb == N_QB - 1 else 1
                win = slice(qb * BLOCK, qb * BLOCK + 3 * BLOCK)
                units.append((kvh, qb, [(k4_ext[win], vt4_ext[:, win], (bias_scr[v, 0], bias_scr[v, 1])),
                                        (k4_ctx, vt4_ctx, None)], snk))
        run_units(units, extras)

    def attend_context(extras):
        units = []
        for sb in range(n_seq_ctx):
            k_all, vt_all = ctx_refs[2 * sb][...], ctx_refs[2 * sb + 1][...]
            for kvh in range(N_KV_HEADS):
                k4, vt4 = keys4(k_all, kvh), values_t(vt_all, kvh)
                snk = sink_row(kvh)
                for qb in range(sb * n_ctx // BLOCK, (sb + 1) * n_ctx // BLOCK):
                    units.append((kvh, qb, [(k4, vt4, None)], snk))
        run_units(units, extras)

    def branch(lat):
        gate = m_ref[0, 2:3, :]
        conv_n = []

        def conv_proj(c):
            cols = slice(c * TN_IN, (c + 1) * TN_IN)
            o_ref[:, cols] = x_ref[:, cols] + gate[:, cols] * _dot(conv_n[0], wout_ref[:D_CONV, cols])

        extras = [lambda: conv_n.append(conv_normed(lat))]
        extras += [functools.partial(conv_proj, c) for c in range(D_MODEL // TN_IN)]
        if lat:
            attend_latent(extras)
        else:
            attend_context(extras)
        a_t = attn_scr[...]
        rstd = lax.rsqrt(jnp.mean(a_t * a_t, axis=0, keepdims=True) + EPS)
        goa_t = jnp.concatenate([goa_ref[...]] * (TM // LANES), axis=1)
        attn_n = (a_t * rstd * goa_t).T.astype(BF16)
        o_ref[...] += gate * _dot(attn_n, wout_ref[D_CONV:, :])

    pl.when(is_lat)(functools.partial(branch, True))
    pl.when(jnp.logical_not(is_lat))(functools.partial(branch, False))


def _mixer(x, p, kd, vt, w_out, layer, conv_w, conv_b, g_oc, g_oa_t, sink, mods, geo, n_ctx, n_out_tiles):
    n_tiles, tpb, n_lat_tiles, n_groups = geo
    n_seq_ctx = TM // n_ctx
    halo_blocks = TM // BF16_SUBLANES
    n_halo, n_kvb = n_tiles * halo_blocks, n_tiles * N_QB
    col_cg, col_hh, col_q = 1, 2, 3
    ctx_blk0 = n_lat_tiles * TM // n_ctx
    seq_of = lambda i: jnp.minimum(i // tpb, n_seq_ctx - 1)
    group = lambda i: (jnp.minimum(i // tpb, n_groups - 1), 0, 0)
    prev_blk = lambda i: jnp.maximum(i * N_QB - 1, 0)
    next_blk = lambda i: jnp.minimum((i + 1) * N_QB, n_kvb - 1)

    def halo(col, nxt):
        if nxt:
            return pl.BlockSpec((BF16_SUBLANES, D_CONV),
                                lambda i: (jnp.minimum((i + 1) * halo_blocks, n_halo - 1), col))
        return pl.BlockSpec((BF16_SUBLANES, D_CONV), lambda i: (jnp.maximum(i * halo_blocks - 1, 0), col))

    in_specs = [
        pl.BlockSpec((TM, D_MODEL), lambda i: (i, 0)),
        pl.BlockSpec((TM, D_CONV), lambda i: (i, 0)),
        pl.BlockSpec((TM, D_CONV), lambda i: (i, col_cg)),
        pl.BlockSpec((TM, D_CONV), lambda i: (i, col_hh)),
        halo(col_cg, False), halo(col_hh, False), halo(col_cg, True), halo(col_hh, True),
        pl.BlockSpec((TM, D_ATTN), lambda i: (i, col_q)),
        pl.BlockSpec((BLOCK, D_KDUP), lambda i: (prev_blk(i), 0)),
        pl.BlockSpec((TM, D_KDUP), lambda i: (i, 0)),
        pl.BlockSpec((BLOCK, D_KDUP), lambda i: (next_blk(i), 0)),
        pl.BlockSpec((D_KV, BLOCK), lambda i: (0, prev_blk(i))),
        pl.BlockSpec((D_KV, TM), lambda i: (0, i)),
        pl.BlockSpec((D_KV, BLOCK), lambda i: (0, next_blk(i))),
        pl.BlockSpec((n_ctx, D_KDUP), lambda i: (ctx_blk0 + seq_of(i), 0)),
        pl.BlockSpec((D_KV, n_ctx), lambda i: (0, ctx_blk0 + seq_of(i))),
    ]
    args = [x, p, p, p, p, p, p, p, p, kd, kd, kd, vt, vt, vt, kd, vt]
    for sb in range(n_seq_ctx):
        in_specs += [pl.BlockSpec((n_ctx, D_KDUP), functools.partial(lambda i, s: (ctx_blk0 + s, 0), s=sb)),
                     pl.BlockSpec((D_KV, n_ctx), functools.partial(lambda i, s: (0, ctx_blk0 + s), s=sb))]
        args += [kd, vt]
    in_specs += [
        pl.BlockSpec((None, D_MODEL, D_MODEL), lambda i: (layer, 0, 0), pipeline_mode=pl.Buffered(1)),
        pl.BlockSpec((3, D_CONV), lambda i: (0, 0)),
        pl.BlockSpec((1, D_CONV), lambda i: (0, 0)),
        pl.BlockSpec((1, D_CONV), lambda i: (0, 0)),
        pl.BlockSpec((D_ATTN, LANES), lambda i: (0, 0)),
        pl.BlockSpec(memory_space=pltpu.SMEM),
        pl.BlockSpec((1, N_MOD, D_MODEL), group),
    ]
    args += [w_out, conv_w, conv_b, g_oc, g_oa_t, sink, mods]
    kern = functools.partial(_mixer_kernel, n_lat_tiles=n_lat_tiles, tpb=tpb, n_ctx=n_ctx, n_seq_ctx=n_seq_ctx)
    return pl.pallas_call(
        kern,
        grid=(n_out_tiles,),
        in_specs=in_specs,
        out_specs=pl.BlockSpec((TM, D_MODEL), lambda i: (i, 0)),
        out_shape=jax.ShapeDtypeStruct((n_out_tiles * TM, D_MODEL), F32),
        scratch_shapes=[pltpu.VMEM((D_ATTN, TM), F32),
                        pltpu.VMEM((3, 2, BLOCK, GQA_GROUP * BLOCK), F32)],
        compiler_params=_params(("arbitrary",)),
        name="mixer",
    )(*args)


def _mlp_kernel(xc_ref, xn_ref, g_ref, mc_ref, mn_ref, w1_ref, w2_ref, gf_ref, o_ref, h_even, h_odd, *, final):
    i, f = pl.program_id(0), pl.program_id(1)
    n_f = D_FF // TF
    rows_per = TM // n_f

    @pl.when((i == 0) & (f == 0))
    def _():
        h_even[...] = _norm_mod(xc_ref[...], g_ref[...], mc_ref[0], 3, 4)

    @pl.when(f == 0)
    def _():
        o_ref[...] = xc_ref[...]

    for parity, (h_cur, h_nxt) in enumerate(((h_even, h_odd), (h_odd, h_even))):
        @pl.when(i % 2 == parity)
        def _():
            r0 = pl.multiple_of(f * rows_per, rows_per)
            h_nxt[pl.ds(r0, rows_per), :] = _norm_mod(xn_ref[pl.ds(r0, rows_per), :], g_ref[...], mn_ref[0], 3, 4)
            hid = jnp.square(jnp.maximum(_dot(h_cur[...], w1_ref[...]), 0.0)).astype(BF16)
            o_ref[...] += mc_ref[0, 5:6, :] * _dot(hid, w2_ref[...])

    if final:
        @pl.when(f == n_f - 1)
        def _():
            o_ref[...] = _rms(o_ref[...]) * gf_ref[...]


def _mlp(x, g, mods, w1, w2, layer, g_final, geo, n_out_tiles, final):
    n_tiles, tpb, n_lat_tiles, n_groups = geo
    nxt = lambda i: jnp.minimum(i + 1, n_out_tiles - 1)
    group = lambda i, f: (jnp.minimum(i // tpb, n_groups - 1), 0, 0)
    group_nxt = lambda i, f: (jnp.minimum(nxt(i) // tpb, n_groups - 1), 0, 0)
    return pl.pallas_call(
        functools.partial(_mlp_kernel, final=final),
        grid=(n_out_tiles, D_FF // TF),
        in_specs=[
            pl.BlockSpec((TM, D_MODEL), lambda i, f: (i, 0)),
            pl.BlockSpec((TM, D_MODEL), lambda i, f: (nxt(i), 0)),
            pl.BlockSpec((1, D_MODEL), lambda i, f: (0, 0)),
            pl.BlockSpec((1, N_MOD, D_MODEL), group),
            pl.BlockSpec((1, N_MOD, D_MODEL), group_nxt),
            pl.BlockSpec((None, D_MODEL, TF), lambda i, f: (layer, 0, f)),
            pl.BlockSpec((None, TF, D_MODEL), lambda i, f: (layer, f, 0)),
            pl.BlockSpec((1, D_MODEL), lambda i, f: (0, 0)),
        ],
        out_specs=pl.BlockSpec((TM, D_MODEL), lambda i, f: (i, 0)),
        out_shape=jax.ShapeDtypeStruct((n_out_tiles * TM, D_MODEL), F32),
        scratch_shapes=[pltpu.VMEM((TM, D_MODEL), BF16), pltpu.VMEM((TM, D_MODEL), BF16)],
        compiler_params=_params(("arbitrary", "arbitrary")),
        name="mlp",
    )(x, x, g, mods, mods, w1, w2, g_final)


def _rope_tables(n_tok):
    rows = n_tok // GRID_W
    row_pos = jnp.repeat(jnp.arange(rows, dtype=F32), GRID_W)
    col_pos = jnp.tile(jnp.arange(GRID_W, dtype=F32), rows)
    inv = ROPE_THETA ** (-jnp.arange(0, ROPE_AXIS_DIM, 2, dtype=F32) / ROPE_AXIS_DIM)
    ang_r = row_pos[:, None] * inv[None, :]
    ang_c = col_pos[:, None] * inv[None, :]
    zeros = jnp.zeros_like(ang_r)
    cos_h = jnp.concatenate([jnp.cos(ang_r)] * 2 + [jnp.cos(ang_c)] * 2, axis=-1)
    hi_h = jnp.concatenate([zeros, jnp.sin(ang_r), zeros, jnp.sin(ang_c)], axis=-1)
    lo_h = jnp.concatenate([-jnp.sin(ang_r), zeros, -jnp.sin(ang_c), zeros], axis=-1)
    reps = LANES // HEAD_DIM
    pad = lambda t, fill: jnp.concatenate([jnp.tile(t, (1, reps)), jnp.full((TM, LANES), fill, F32)], axis=0)
    return pad(cos_h, 1.0), pad(hi_h, 0.0), pad(lo_h, 0.0)


def kernel(x, c, ctx, c_ctx, w_ada, b_ada, g_norm1, g_norm2, w_in, conv_w, conv_b, sink,
           g_out_conv, g_out_attn, w_out, w_mlp1, w_mlp2, g_final):
    bsz, n_tok, d = x.shape
    n_ctx = ctx.shape[1]
    depth = w_ada.shape[0]
    assert d == D_MODEL and n_tok % TM == 0 and n_tok % GRID_W == 0
    assert bsz * n_ctx == TM and n_ctx % BLOCK == 0 and bsz + 1 <= MOD_ROWS
    tpb = n_tok // TM
    n_lat_tiles = bsz * tpb
    n_tiles = n_lat_tiles + 1
    geo = (n_tiles, tpb, n_lat_tiles, bsz + 1)

    cc = jnp.concatenate([c, c_ctx[None, :], jnp.zeros((MOD_ROWS - bsz - 1, d), F32)], axis=0)
    mods = _ada(cc, w_ada, b_ada).reshape(depth, MOD_ROWS, N_MOD, d)
    tabs = _rope_tables(n_tok)
    xs = jnp.concatenate([x.reshape(bsz * n_tok, d), ctx.reshape(bsz * n_ctx, d)], axis=0)
    row = lambda v: v.reshape(1, -1)
    w_in, w_out, w_mlp1, w_mlp2 = (w.astype(BF16) for w in (w_in, w_out, w_mlp1, w_mlp2))

    for i in range(depth):
        last = i == depth - 1
        n_out = n_lat_tiles if last else n_tiles
        p, kd, vt = _inproj(xs, row(g_norm1[i]), mods[i], w_in, i, tabs, geo)
        g_oa_t = jnp.broadcast_to(g_out_attn[i][:, None], (D_ATTN, LANES))
        xs = _mixer(xs, p, kd, vt, w_out, i, conv_w[i], row(conv_b[i]), row(g_out_conv[i]),
                    g_oa_t, sink[i], mods[i], geo, n_ctx, n_out)
        xs = _mlp(xs, row(g_norm2[i]), mods[i], w_mlp1, w_mlp2, i, row(g_final), geo, n_out, last)
    return xs.reshape(bsz, n_tok, d)
```

```python
import functools

import jax
import jax.numpy as jnp
from jax import lax
from jax.experimental import pallas as pl
from jax.experimental.pallas import tpu as pltpu

D_MODEL = 2048
D_CONV = D_MODEL // 2
D_ATTN = D_MODEL - D_CONV
HEAD_DIM = 64
N_HEADS = D_ATTN // HEAD_DIM
N_KV_HEADS = 4
GQA_GROUP = N_HEADS // N_KV_HEADS
D_KV = N_KV_HEADS * HEAD_DIM
D_QG = GQA_GROUP * HEAD_DIM
GRID_W = 64
WINDOW = 128
BLOCK = 128
D_FF = 4 * D_MODEL
ROPE_THETA = 10000.0
ROPE_AXIS_DIM = HEAD_DIM // 2
ROPE_HALF = ROPE_AXIS_DIM // 2
EPS = 1e-6
N_MOD = 6
KV_START = 3 * D_CONV + D_ATTN
D_IN_PROJ = KV_START + 2 * D_KV
SCALE = HEAD_DIM ** -0.5
LOG2E = 1.4426950408889634
Q_SCALE = SCALE * LOG2E
NEG_INF = -1e30

LANES = 128
BF16_SUBLANES = 16
MOD_ROWS = 8
TM = 512
TN_IN = 512
TF = 1024
TN_ADA = 1024
D_KDUP = N_KV_HEADS * LANES
P_CG, P_HH, P_BG, P_Q = 0, D_CONV, 2 * D_CONV, 3 * D_CONV
N_QB = TM // BLOCK
EXTRAS_FROM = 3
UNITS_AHEAD = 2
VMEM_LIMIT = 56 * 1024 * 1024

F32 = jnp.float32
BF16 = jnp.bfloat16


def _params(sem):
    return pltpu.CompilerParams(dimension_semantics=sem, vmem_limit_bytes=VMEM_LIMIT)


def _rms(v):
    return v * lax.rsqrt(jnp.mean(v * v, axis=-1, keepdims=True) + EPS)


def _dot(a, b):
    return jnp.dot(a, b, preferred_element_type=F32)


def _dot_nt(a, b):
    return lax.dot_general(a, b, (((1,), (1,)), ((), ())), preferred_element_type=F32)


def _norm_mod(x, g, m, shift_row, scale_row):
    y = _rms(x) * g
    return (y * (1 + m[scale_row:scale_row + 1, :]) + m[shift_row:shift_row + 1, :]).astype(BF16)


def _ada_kernel(c_ref, w_ref, b_ref, o_ref):
    c = c_ref[...]
    s = (c * jax.nn.sigmoid(c)).astype(BF16)
    o_ref[0] = _dot(s, w_ref[0].astype(BF16)) + b_ref[0]


def _ada(cc, w_ada, b_ada):
    depth = w_ada.shape[0]
    n_out = N_MOD * D_MODEL
    return pl.pallas_call(
        _ada_kernel,
        grid=(depth, n_out // TN_ADA),
        in_specs=[
            pl.BlockSpec((MOD_ROWS, D_MODEL), lambda l, j: (0, 0)),
            pl.BlockSpec((1, D_MODEL, TN_ADA), lambda l, j: (l, 0, j)),
            pl.BlockSpec((1, 1, TN_ADA), lambda l, j: (l, 0, j)),
        ],
        out_specs=pl.BlockSpec((1, MOD_ROWS, TN_ADA), lambda l, j: (l, 0, j)),
        out_shape=jax.ShapeDtypeStruct((depth, MOD_ROWS, n_out), F32),
        compiler_params=_params(("arbitrary", "arbitrary")),
        name="adaln",
    )(cc, w_ada, b_ada.reshape(depth, 1, n_out))


def _rope(a, cos, sin_hi, sin_lo):
    outs = []
    for c in range(a.shape[1] // LANES):
        v = a[:, c * LANES:(c + 1) * LANES]
        outs.append(v * cos + pltpu.roll(v, ROPE_HALF, 1) * sin_hi
                    + pltpu.roll(v, LANES - ROPE_HALF, 1) * sin_lo)
    return outs


def _inproj_body(h_ref, h_nxt, xn_ref, g_ref, mn_ref, w_ref, cos_ref, shi_ref, slo_ref, p_ref, kd_ref, vt_ref):
    h = h_ref[...]
    cos, shi, slo = cos_ref[...], shi_ref[...], slo_ref[...]
    n_chunks = D_IN_PROJ // TN_IN
    q_first = (3 * D_CONV) // TN_IN
    kv_tile = KV_START // TN_IN
    rows_per = TM // (n_chunks - 1)
    low_half = lax.broadcasted_iota(jnp.int32, (TM, LANES), 1) < HEAD_DIM
    order = [kv_tile] + list(range(q_first, kv_tile)) + list(range(q_first))
    for step, c in enumerate(order):
        cols = slice(c * TN_IN, (c + 1) * TN_IN)
        acc = _dot(h, w_ref[:, cols])
        if step < n_chunks - 1:
            rows = slice(step * rows_per, (step + 1) * rows_per)
            h_nxt[rows, :] = _norm_mod(xn_ref[rows, :], g_ref[...], mn_ref[0], 0, 1)
        if c < q_first:
            o0 = (P_BG if c * TN_IN < D_CONV else P_CG - D_CONV) + c * TN_IN
            p_ref[:, o0:o0 + TN_IN] = acc.astype(BF16)
        elif c < kv_tile:
            o0 = P_Q + (c - q_first) * TN_IN
            for k, v in enumerate(_rope(acc, cos, shi, slo)):
                p_ref[:, o0 + k * LANES:o0 + (k + 1) * LANES] = (v * Q_SCALE).astype(BF16)
        else:
            for k, v in enumerate(_rope(acc[:, :D_KV], cos, shi, slo)):
                swapped = pltpu.roll(v, HEAD_DIM, 1)
                kd_ref[:, (2 * k) * LANES:(2 * k + 1) * LANES] = jnp.where(low_half, v, swapped).astype(BF16)
                kd_ref[:, (2 * k + 1) * LANES:(2 * k + 2) * LANES] = jnp.where(low_half, swapped, v).astype(BF16)
            vt_ref[...] = acc[:, D_KV:].T.astype(BF16)


def _inproj_kernel(x0_ref, xn_ref, g_ref, m0_ref, mn_ref, w_ref, cos_ref, shi_ref, slo_ref,
                   p_ref, kd_ref, vt_ref, h_even, h_odd):
    i = pl.program_id(0)

    @pl.when(i == 0)
    def _():
        h_even[...] = _norm_mod(x0_ref[...], g_ref[...], m0_ref[0], 0, 1)

    for parity, (h_cur, h_nxt) in enumerate(((h_even, h_odd), (h_odd, h_even))):
        @pl.when(i % 2 == parity)
        def _():
            _inproj_body(h_cur, h_nxt, xn_ref, g_ref, mn_ref, w_ref, cos_ref, shi_ref, slo_ref,
                         p_ref, kd_ref, vt_ref)


def _inproj(x, g, mods, w, layer, tabs, geo):
    n_tiles, tpb, n_lat_tiles, n_groups = geo
    nxt = lambda i: jnp.minimum(i + 1, n_tiles - 1)
    group_nxt = lambda i: (jnp.minimum(nxt(i) // tpb, n_groups - 1), 0, 0)
    tab_idx = lambda i: (jnp.where(i < n_lat_tiles, i % tpb, tpb), 0)
    once = pl.Buffered(1)
    rows = n_tiles * TM
    return pl.pallas_call(
        _inproj_kernel,
        grid=(n_tiles,),
        in_specs=[
            pl.BlockSpec((TM, D_MODEL), lambda i: (0, 0), pipeline_mode=once),
            pl.BlockSpec((TM, D_MODEL), lambda i: (nxt(i), 0)),
            pl.BlockSpec((1, D_MODEL), lambda i: (0, 0)),
            pl.BlockSpec((1, N_MOD, D_MODEL), lambda i: (0, 0, 0)),
            pl.BlockSpec((1, N_MOD, D_MODEL), group_nxt),
            pl.BlockSpec((None, D_MODEL, D_IN_PROJ), lambda i: (layer, 0, 0), pipeline_mode=once),
            pl.BlockSpec((TM, LANES), tab_idx),
            pl.BlockSpec((TM, LANES), tab_idx),
            pl.BlockSpec((TM, LANES), tab_idx),
        ],
        out_specs=[
            pl.BlockSpec((TM, KV_START), lambda i: (i, 0)),
            pl.BlockSpec((TM, D_KDUP), lambda i: (i, 0)),
            pl.BlockSpec((D_KV, TM), lambda i: (0, i)),
        ],
        out_shape=[
            jax.ShapeDtypeStruct((rows, KV_START), BF16),
            jax.ShapeDtypeStruct((rows, D_KDUP), BF16),
            jax.ShapeDtypeStruct((D_KV, rows), BF16),
        ],
        scratch_shapes=[pltpu.VMEM((TM, D_MODEL), BF16), pltpu.VMEM((TM, D_MODEL), BF16)],
        compiler_params=_params(("arbitrary",)),
        name="inproj",
    )(x, x, g, mods, mods, w, *tabs)


def _band_bias(first, last):
    jj = lax.broadcasted_iota(jnp.int32, (3 * BLOCK, GQA_GROUP * BLOCK), 0)
    rr = lax.broadcasted_iota(jnp.int32, (3 * BLOCK, GQA_GROUP * BLOCK), 1) % BLOCK
    ok = (jj >= rr) & (jj <= rr + 2 * WINDOW)
    if first:
        ok = ok & (jj >= BLOCK)
    if last:
        ok = ok & (jj < 2 * BLOCK)
    bias = jnp.where(ok, 0.0, NEG_INF).astype(F32)
    return bias[:BLOCK], bias[2 * BLOCK:]


def _attend_t(q_blk, head_masks, parts, snk):
    return _mix_values(*_scores(q_blk, head_masks, parts, snk), parts, snk)


def _scores(q_blk, head_masks, parts, snk):
    q_bd = jnp.concatenate([q_blk * hm for hm in head_masks], axis=0)
    s = _dot_nt(jnp.concatenate([k4 for k4, _, _ in parts], axis=0), q_bd)
    rows, r0 = [], 0
    for k4, _, bias in parts:
        n = k4.shape[0]
        if bias is None:
            rows.append(s[r0:r0 + n])
        else:
            rows += [s[r0:r0 + BLOCK] + bias[0], s[r0 + BLOCK:r0 + n - BLOCK], s[r0 + n - BLOCK:r0 + n] + bias[1]]
        r0 += n
    s = jnp.concatenate(rows, axis=0)
    return s, jnp.maximum(snk, jnp.max(s, axis=0, keepdims=True))


def _mix_values(s, m, parts, snk):
    acc = _dot(jnp.concatenate([vt for _, vt, _ in parts], axis=1), jnp.exp2(s - m).astype(BF16))
    denom = acc[HEAD_DIM:HEAD_DIM + 1, :] + jnp.exp2(snk - m)
    return acc[:HEAD_DIM, :] / denom


def _set_row(a, row, value):
    g0 = row // MOD_ROWS * MOD_ROWS
    r8 = lax.broadcasted_iota(jnp.int32, (MOD_ROWS, 1), 0)
    piece = jnp.where(r8 == row - g0, value, a[g0:g0 + MOD_ROWS])
    return jnp.concatenate([t for t in (a[:g0], piece, a[g0 + MOD_ROWS:]) if t.shape[0]], axis=0)


def _mixer_kernel(x_ref, p_ref, pp_ref, pn_ref,
                  kp_ref, kc_ref, kn_ref, vp_ref, vc_ref, vn_ref, ck_ref, cv_ref, *rest,
                  n_lat_tiles, tpb, n_ctx, n_seq_ctx):
    ctx_refs = rest[:2 * n_seq_ctx]
    (wout_ref, cw_ref, cb_ref, goc_ref, goa_ref, sink_ref, m_ref, o_ref, attn_scr, bias_scr) = rest[2 * n_seq_ctx:]
    i = pl.program_id(0)
    is_lat = i < n_lat_tiles
    tib = i % tpb

    @pl.when(i == 0)
    def _():
        for v, (first, last) in enumerate(((True, False), (False, False), (False, True))):
            lo, hi = _band_bias(first, last)
            bias_scr[v, 0] = lo
            bias_scr[v, 1] = hi

    lane_head = lax.broadcasted_iota(jnp.int32, (1, D_QG), 1) // HEAD_DIM
    head_masks = [(lane_head == g).astype(F32).astype(BF16) for g in range(GQA_GROUP)]
    ones_rows = jnp.ones((BF16_SUBLANES, TM + 2 * BLOCK), BF16)

    def conv_normed(lat):
        cg, hh = slice(P_CG, P_CG + D_CONV), slice(P_HH, P_HH + D_CONV)
        u = p_ref[:, cg].astype(F32) * p_ref[:, hh].astype(F32)
        u_prev, u_next = pltpu.roll(u, 1, 0), pltpu.roll(u, TM - 1, 0)
        if lat:
            last = BF16_SUBLANES - 1
            prev_edge = pp_ref[last:last + 1, cg].astype(F32) * pp_ref[last:last + 1, hh].astype(F32)
            next_edge = pn_ref[0:1, cg].astype(F32) * pn_ref[0:1, hh].astype(F32)
            u_prev = _set_row(u_prev, 0, jnp.where(tib > 0, prev_edge, 0.0))
            u_next = _set_row(u_next, TM - 1, jnp.where(tib < tpb - 1, next_edge, 0.0))
        else:
            for sb in range(n_seq_ctx):
                u_prev = _set_row(u_prev, sb * n_ctx, 0.0)
                u_next = _set_row(u_next, (sb + 1) * n_ctx - 1, 0.0)
        conv = u_prev * cw_ref[0:1, :] + u * cw_ref[1:2, :] + u_next * cw_ref[2:3, :] + cb_ref[...]
        return (_rms(p_ref[:, P_BG:P_BG + D_CONV].astype(F32) * conv) * goc_ref[...]).astype(BF16)

    def sink_row(kvh):
        return jnp.concatenate(
            [jnp.full((1, BLOCK), sink_ref[kvh * GQA_GROUP + g] * LOG2E, F32) for g in range(GQA_GROUP)], axis=1)

    def keys4(kd, kvh):
        k2 = kd[:, kvh * LANES:(kvh + 1) * LANES]
        return jnp.concatenate([k2] * (D_QG // LANES), axis=1)

    def values_t(vt, kvh):
        return jnp.concatenate([vt[kvh * HEAD_DIM:(kvh + 1) * HEAD_DIM, :], ones_rows[:, :vt.shape[1]]], axis=0)

    def put(kvh, qb, o_t):
        for g in range(GQA_GROUP):
            r0 = (kvh * GQA_GROUP + g) * HEAD_DIM
            attn_scr[r0:r0 + HEAD_DIM, qb * BLOCK:(qb + 1) * BLOCK] = o_t[:, g * BLOCK:(g + 1) * BLOCK]

    def run_units(units, extras):
        extras = list(extras)
        every = max(1, (len(units) - EXTRAS_FROM) // max(1, len(extras)))
        in_flight = []
        for u, unit in enumerate(units + [None] * UNITS_AHEAD):
            if unit is not None:
                kvh, qb, parts, snk = unit
                q_blk = p_ref[qb * BLOCK:(qb + 1) * BLOCK, P_Q + kvh * D_QG:P_Q + (kvh + 1) * D_QG]
                in_flight.append((kvh, qb, parts, snk, _scores(q_blk, head_masks, parts, snk)))
            if u >= UNITS_AHEAD:
                kvh, qb, parts, snk, (s, m) = in_flight.pop(0)
                put(kvh, qb, _mix_values(s, m, parts, snk))
            if extras and u >= EXTRAS_FROM and (u - EXTRAS_FROM) % every == 0:
                extras.pop(0)()
        for extra in extras:
            extra()

    def attend_latent(extras):
        kd_ext = jnp.concatenate([kp_ref[...], kc_ref[...], kn_ref[...]], axis=0)
        vt_ext = jnp.concatenate([vp_ref[...], vc_ref[...], vn_ref[...]], axis=1)
        first_idx = jnp.where(tib == 0, 0, 1)
        last_idx = jnp.where(tib == tpb - 1, 2, 1)
        units = []
        for kvh in range(N_KV_HEADS):
            k4_ext, vt4_ext = keys4(kd_ext, kvh), values_t(vt_ext, kvh)
            k4_ctx, vt4_ctx = keys4(ck_ref[...], kvh), values_t(cv_ref[...], kvh)
            snk = sink_row(kvh)
            for qb in range(N_QB):
                v = first_idx if qb == 0 else last_idx if qb == N_QB - 1 else 1
                win = slice(qb * BLOCK, qb * BLOCK + 3 * BLOCK)
                units.append((kvh, qb, [(k4_ext[win], vt4_ext[:, win], (bias_scr[v, 0], bias_scr[v, 1])),
                                        (k4_ctx, vt4_ctx, None)], snk))
        run_units(units, extras)

    def attend_context(extras):
        units = []
        for sb in range(n_seq_ctx):
            k_all, vt_all = ctx_refs[2 * sb][...], ctx_refs[2 * sb + 1][...]
            for kvh in range(N_KV_HEADS):
                k4, vt4 = keys4(k_all, kvh), values_t(vt_all, kvh)
                snk = sink_row(kvh)
                for qb in range(sb * n_ctx // BLOCK, (sb + 1) * n_ctx // BLOCK):
                    units.append((kvh, qb, [(k4, vt4, None)], snk))
        run_units(units, extras)

    def branch(lat):
        gate = m_ref[0, 2:3, :]
        conv_n = []

        def conv_proj(c):
            cols = slice(c * TN_IN, (c + 1) * TN_IN)
            o_ref[:, cols] = x_ref[:, cols] + gate[:, cols] * _dot(conv_n[0], wout_ref[:D_CONV, cols])

        extras = [lambda: conv_n.append(conv_normed(lat))]
        extras += [functools.partial(conv_proj, c) for c in range(D_MODEL // TN_IN)]
        if lat:
            attend_latent(extras)
        else:
            attend_context(extras)
        a_t = attn_scr[...]
        rstd = lax.rsqrt(jnp.mean(a_t * a_t, axis=0, keepdims=True) + EPS)
        goa_t = jnp.concatenate([goa_ref[...]] * (TM // LANES), axis=1)
        attn_n = (a_t * rstd * goa_t).T.astype(BF16)
        o_ref[...] += gate * _dot(attn_n, wout_ref[D_CONV:, :])

    pl.when(is_lat)(functools.partial(branch, True))
    pl.when(jnp.logical_not(is_lat))(functools.partial(branch, False))


def _mixer(x, p, kd, vt, w_out, layer, conv_w, conv_b, g_oc, g_oa_t, sink, mods, geo, n_ctx, n_out_tiles):
    n_tiles, tpb, n_lat_tiles, n_groups = geo
    n_seq_ctx = TM // n_ctx
    halo_blocks = TM // BF16_SUBLANES
    n_halo, n_kvb = n_tiles * halo_blocks, n_tiles * N_QB
    ctx_blk0 = n_lat_tiles * TM // n_ctx
    seq_of = lambda i: jnp.minimum(i // tpb, n_seq_ctx - 1)
    group = lambda i: (jnp.minimum(i // tpb, n_groups - 1), 0, 0)
    prev_blk = lambda i: jnp.maximum(i * N_QB - 1, 0)
    next_blk = lambda i: jnp.minimum((i + 1) * N_QB, n_kvb - 1)

    halo_prev = pl.BlockSpec((BF16_SUBLANES, 2 * D_CONV), lambda i: (jnp.maximum(i * halo_blocks - 1, 0), 0))
    halo_next = pl.BlockSpec((BF16_SUBLANES, 2 * D_CONV),
                             lambda i: (jnp.minimum((i + 1) * halo_blocks, n_halo - 1), 0))
    in_specs = [
        pl.BlockSpec((TM, D_MODEL), lambda i: (i, 0)),
        pl.BlockSpec((TM, KV_START), lambda i: (i, 0)),
        halo_prev, halo_next,
        pl.BlockSpec((BLOCK, D_KDUP), lambda i: (prev_blk(i), 0)),
        pl.BlockSpec((TM, D_KDUP), lambda i: (i, 0)),
        pl.BlockSpec((BLOCK, D_KDUP), lambda i: (next_blk(i), 0)),
        pl.BlockSpec((D_KV, BLOCK), lambda i: (0, prev_blk(i))),
        pl.BlockSpec((D_KV, TM), lambda i: (0, i)),
        pl.BlockSpec((D_KV, BLOCK), lambda i: (0, next_blk(i))),
        pl.BlockSpec((n_ctx, D_KDUP), lambda i: (ctx_blk0 + seq_of(i), 0)),
        pl.BlockSpec((D_KV, n_ctx), lambda i: (0, ctx_blk0 + seq_of(i))),
    ]
    args = [x, p, p, p, kd, kd, kd, vt, vt, vt, kd, vt]
    for sb in range(n_seq_ctx):
        in_specs += [pl.BlockSpec((n_ctx, D_KDUP), functools.partial(lambda i, s: (ctx_blk0 + s, 0), s=sb)),
                     pl.BlockSpec((D_KV, n_ctx), functools.partial(lambda i, s: (0, ctx_blk0 + s), s=sb))]
        args += [kd, vt]
    in_specs += [
        pl.BlockSpec((None, D_MODEL, D_MODEL), lambda i: (layer, 0, 0), pipeline_mode=pl.Buffered(1)),
        pl.BlockSpec((3, D_CONV), lambda i: (0, 0)),
        pl.BlockSpec((1, D_CONV), lambda i: (0, 0)),
        pl.BlockSpec((1, D_CONV), lambda i: (0, 0)),
        pl.BlockSpec((D_ATTN, LANES), lambda i: (0, 0)),
        pl.BlockSpec(memory_space=pltpu.SMEM),
        pl.BlockSpec((1, N_MOD, D_MODEL), group),
    ]
    args += [w_out, conv_w, conv_b, g_oc, g_oa_t, sink, mods]
    kern = functools.partial(_mixer_kernel, n_lat_tiles=n_lat_tiles, tpb=tpb, n_ctx=n_ctx, n_seq_ctx=n_seq_ctx)
    return pl.pallas_call(
        kern,
        grid=(n_out_tiles,),
        in_specs=in_specs,
        out_specs=pl.BlockSpec((TM, D_MODEL), lambda i: (i, 0)),
        out_shape=jax.ShapeDtypeStruct((n_out_tiles * TM, D_MODEL), F32),
        scratch_shapes=[pltpu.VMEM((D_ATTN, TM), F32),
                        pltpu.VMEM((3, 2, BLOCK, GQA_GROUP * BLOCK), F32)],
        compiler_params=_params(("arbitrary",)),
        name="mixer",
    )(*args)


def _mlp_kernel(xc_ref, xn_ref, g_ref, mc_ref, mn_ref, w1_ref, w2_ref, gf_ref, o_ref, h_even, h_odd, *, final):
    i, f = pl.program_id(0), pl.program_id(1)
    n_f = D_FF // TF
    rows_per = TM // n_f

    @pl.when((i == 0) & (f == 0))
    def _():
        h_even[...] = _norm_mod(xc_ref[...], g_ref[...], mc_ref[0], 3, 4)

    @pl.when(f == 0)
    def _():
        o_ref[...] = xc_ref[...]

    for parity, (h_cur, h_nxt) in enumerate(((h_even, h_odd), (h_odd, h_even))):
        @pl.when(i % 2 == parity)
        def _():
            r0 = pl.multiple_of(f * rows_per, rows_per)
            h_nxt[pl.ds(r0, rows_per), :] = _norm_mod(xn_ref[pl.ds(r0, rows_per), :], g_ref[...], mn_ref[0], 3, 4)
            hid = jnp.square(jnp.maximum(_dot(h_cur[...], w1_ref[...]), 0.0)).astype(BF16)
            o_ref[...] += mc_ref[0, 5:6, :] * _dot(hid, w2_ref[...])

    if final:
        @pl.when(f == n_f - 1)
        def _():
            o_ref[...] = _rms(o_ref[...]) * gf_ref[...]


def _mlp(x, g, mods, w1, w2, layer, g_final, geo, n_out_tiles, final):
    n_tiles, tpb, n_lat_tiles, n_groups = geo
    nxt = lambda i: jnp.minimum(i + 1, n_out_tiles - 1)
    group = lambda i, f: (jnp.minimum(i // tpb, n_groups - 1), 0, 0)
    group_nxt = lambda i, f: (jnp.minimum(nxt(i) // tpb, n_groups - 1), 0, 0)
    return pl.pallas_call(
        functools.partial(_mlp_kernel, final=final),
        grid=(n_out_tiles, D_FF // TF),
        in_specs=[
            pl.BlockSpec((TM, D_MODEL), lambda i, f: (i, 0)),
            pl.BlockSpec((TM, D_MODEL), lambda i, f: (nxt(i), 0)),
            pl.BlockSpec((1, D_MODEL), lambda i, f: (0, 0)),
            pl.BlockSpec((1, N_MOD, D_MODEL), group),
            pl.BlockSpec((1, N_MOD, D_MODEL), group_nxt),
            pl.BlockSpec((None, D_MODEL, TF), lambda i, f: (layer, 0, f)),
            pl.BlockSpec((None, TF, D_MODEL), lambda i, f: (layer, f, 0)),
            pl.BlockSpec((1, D_MODEL), lambda i, f: (0, 0)),
        ],
        out_specs=pl.BlockSpec((TM, D_MODEL), lambda i, f: (i, 0)),
        out_shape=jax.ShapeDtypeStruct((n_out_tiles * TM, D_MODEL), F32),
        scratch_shapes=[pltpu.VMEM((TM, D_MODEL), BF16), pltpu.VMEM((TM, D_MODEL), BF16)],
        compiler_params=_params(("arbitrary", "arbitrary")),
        name="mlp",
    )(x, x, g, mods, mods, w1, w2, g_final)


def _rope_tables(n_tok):
    rows = n_tok // GRID_W
    row_pos = jnp.repeat(jnp.arange(rows, dtype=F32), GRID_W)
    col_pos = jnp.tile(jnp.arange(GRID_W, dtype=F32), rows)
    inv = ROPE_THETA ** (-jnp.arange(0, ROPE_AXIS_DIM, 2, dtype=F32) / ROPE_AXIS_DIM)
    ang_r = row_pos[:, None] * inv[None, :]
    ang_c = col_pos[:, None] * inv[None, :]
    zeros = jnp.zeros_like(ang_r)
    cos_h = jnp.concatenate([jnp.cos(ang_r)] * 2 + [jnp.cos(ang_c)] * 2, axis=-1)
    hi_h = jnp.concatenate([zeros, jnp.sin(ang_r), zeros, jnp.sin(ang_c)], axis=-1)
    lo_h = jnp.concatenate([-jnp.sin(ang_r), zeros, -jnp.sin(ang_c), zeros], axis=-1)
    reps = LANES // HEAD_DIM
    pad = lambda t, fill: jnp.concatenate([jnp.tile(t, (1, reps)), jnp.full((TM, LANES), fill, F32)], axis=0)
    return pad(cos_h, 1.0), pad(hi_h, 0.0), pad(lo_h, 0.0)


def kernel(x, c, ctx, c_ctx, w_ada, b_ada, g_norm1, g_norm2, w_in, conv_w, conv_b, sink,
           g_out_conv, g_out_attn, w_out, w_mlp1, w_mlp2, g_final):
    bsz, n_tok, d = x.shape
    n_ctx = ctx.shape[1]
    depth = w_ada.shape[0]
    assert d == D_MODEL and n_tok % TM == 0 and n_tok % GRID_W == 0
    assert bsz * n_ctx == TM and n_ctx % BLOCK == 0 and bsz + 1 <= MOD_ROWS
    tpb = n_tok // TM
    n_lat_tiles = bsz * tpb
    n_tiles = n_lat_tiles + 1
    geo = (n_tiles, tpb, n_lat_tiles, bsz + 1)

    cc = jnp.concatenate([c, c_ctx[None, :], jnp.zeros((MOD_ROWS - bsz - 1, d), F32)], axis=0)
    mods = _ada(cc, w_ada, b_ada).reshape(depth, MOD_ROWS, N_MOD, d)
    tabs = _rope_tables(n_tok)
    xs = jnp.concatenate([x.reshape(bsz * n_tok, d), ctx.reshape(bsz * n_ctx, d)], axis=0)
    row = lambda v: v.reshape(1, -1)
    w_in, w_out, w_mlp1, w_mlp2 = (w.astype(BF16) for w in (w_in, w_out, w_mlp1, w_mlp2))

    for i in range(depth):
        last = i == depth - 1
        n_out = n_lat_tiles if last else n_tiles
        p, kd, vt = _inproj(xs, row(g_norm1[i]), mods[i], w_in, i, tabs, geo)
        g_oa_t = jnp.broadcast_to(g_out_attn[i][:, None], (D_ATTN, LANES))
        xs = _mixer(xs, p, kd, vt, w_out, i, conv_w[i], row(conv_b[i]), row(g_out_conv[i]),
                    g_oa_t, sink[i], mods[i], geo, n_ctx, n_out)
        xs = _mlp(xs, row(g_norm2[i]), mods[i], w_mlp1, w_mlp2, i, row(g_final), geo, n_out, last)
    return xs.reshape(bsz, n_tok, d)
```

```python
import functools

import jax
import jax.numpy as jnp
from jax import lax
from jax.experimental import pallas as pl
from jax.experimental.pallas import tpu as pltpu

D_MODEL = 2048
D_CONV = D_MODEL // 2
D_ATTN = D_MODEL - D_CONV
HEAD_DIM = 64
N_HEADS = D_ATTN // HEAD_DIM
N_KV_HEADS = 4
GQA_GROUP = N_HEADS // N_KV_HEADS
D_KV = N_KV_HEADS * HEAD_DIM
D_QG = GQA_GROUP * HEAD_DIM
GRID_W = 64
WINDOW = 128
BLOCK = 128
D_FF = 4 * D_MODEL
ROPE_THETA = 10000.0
ROPE_AXIS_DIM = HEAD_DIM // 2
ROPE_HALF = ROPE_AXIS_DIM // 2
EPS = 1e-6
N_MOD = 6
KV_START = 3 * D_CONV + D_ATTN
D_IN_PROJ = KV_START + 2 * D_KV
SCALE = HEAD_DIM ** -0.5
LOG2E = 1.4426950408889634
Q_SCALE = SCALE * LOG2E
NEG_INF = -1e30

LANES = 128
BF16_SUBLANES = 16
MOD_ROWS = 8
TM = 512
TN_IN = 512
TF = 1024
TN_ADA = 1024
D_KDUP = N_KV_HEADS * LANES
P_CG, P_HH, P_BG, P_Q = 0, D_CONV, 2 * D_CONV, 3 * D_CONV
N_QB = TM // BLOCK
EXTRAS_FROM = 3
UNITS_AHEAD = 2
VMEM_LIMIT = 56 * 1024 * 1024

F32 = jnp.float32
BF16 = jnp.bfloat16


def _params(sem):
    return pltpu.CompilerParams(dimension_semantics=sem, vmem_limit_bytes=VMEM_LIMIT)


def _rms(v):
    return v * lax.rsqrt(jnp.mean(v * v, axis=-1, keepdims=True) + EPS)


def _dot(a, b):
    return jnp.dot(a, b, preferred_element_type=F32)


def _dot_nt(a, b):
    return lax.dot_general(a, b, (((1,), (1,)), ((), ())), preferred_element_type=F32)


def _norm_mod(x, g, m, shift_row, scale_row):
    y = _rms(x) * g
    return (y * (1 + m[scale_row:scale_row + 1, :]) + m[shift_row:shift_row + 1, :]).astype(BF16)


def _ada_kernel(c_ref, w_ref, b_ref, o_ref):
    c = c_ref[...]
    s = (c * jax.nn.sigmoid(c)).astype(BF16)
    o_ref[0] = _dot(s, w_ref[0].astype(BF16)) + b_ref[0]


def _ada(cc, w_ada, b_ada):
    depth = w_ada.shape[0]
    n_out = N_MOD * D_MODEL
    return pl.pallas_call(
        _ada_kernel,
        grid=(depth, n_out // TN_ADA),
        in_specs=[
            pl.BlockSpec((MOD_ROWS, D_MODEL), lambda l, j: (0, 0)),
            pl.BlockSpec((1, D_MODEL, TN_ADA), lambda l, j: (l, 0, j)),
            pl.BlockSpec((1, 1, TN_ADA), lambda l, j: (l, 0, j)),
        ],
        out_specs=pl.BlockSpec((1, MOD_ROWS, TN_ADA), lambda l, j: (l, 0, j)),
        out_shape=jax.ShapeDtypeStruct((depth, MOD_ROWS, n_out), F32),
        compiler_params=_params(("arbitrary", "arbitrary")),
        name="adaln",
    )(cc, w_ada, b_ada.reshape(depth, 1, n_out))


def _rope(a, cos, sin_hi, sin_lo):
    outs = []
    for c in range(a.shape[1] // LANES):
        v = a[:, c * LANES:(c + 1) * LANES]
        outs.append(v * cos + pltpu.roll(v, ROPE_HALF, 1) * sin_hi
                    + pltpu.roll(v, LANES - ROPE_HALF, 1) * sin_lo)
    return outs


def _inproj_body(h_ref, h_nxt, next_rows, g_ref, mn_ref, w_ref, cos_ref, shi_ref, slo_ref, p_ref, kd_ref, vt_ref):
    h = h_ref[...]
    cos, shi, slo = cos_ref[...], shi_ref[...], slo_ref[...]
    n_chunks = D_IN_PROJ // TN_IN
    q_first = (3 * D_CONV) // TN_IN
    kv_tile = KV_START // TN_IN
    rows_per = TM // (n_chunks - 1)
    low_half = lax.broadcasted_iota(jnp.int32, (TM, LANES), 1) < HEAD_DIM
    order = [kv_tile] + list(range(q_first, kv_tile)) + list(range(q_first))
    for step, c in enumerate(order):
        cols = slice(c * TN_IN, (c + 1) * TN_IN)
        acc = _dot(h, w_ref[:, cols])
        if step < n_chunks - 1:
            rows = slice(step * rows_per, (step + 1) * rows_per)
            h_nxt[rows, :] = _norm_mod(next_rows(rows), g_ref[...], mn_ref[0], 0, 1)
        if c < q_first:
            o0 = (P_BG if c * TN_IN < D_CONV else P_CG - D_CONV) + c * TN_IN
            p_ref[:, o0:o0 + TN_IN] = acc.astype(BF16)
        elif c < kv_tile:
            o0 = P_Q + (c - q_first) * TN_IN
            for k, v in enumerate(_rope(acc, cos, shi, slo)):
                p_ref[:, o0 + k * LANES:o0 + (k + 1) * LANES] = (v * Q_SCALE).astype(BF16)
        else:
            for k, v in enumerate(_rope(acc[:, :D_KV], cos, shi, slo)):
                swapped = pltpu.roll(v, HEAD_DIM, 1)
                kd_ref[:, (2 * k) * LANES:(2 * k + 1) * LANES] = jnp.where(low_half, v, swapped).astype(BF16)
                kd_ref[:, (2 * k + 1) * LANES:(2 * k + 2) * LANES] = jnp.where(low_half, swapped, v).astype(BF16)
            vt_ref[...] = acc[:, D_KV:].T.astype(BF16)


def _inproj_kernel(x0_ref, xn_ref, *rest, n_lat_tiles, split):
    xctx_ref = rest[0] if split else None
    (g_ref, m0_ref, mn_ref, w_ref, cos_ref, shi_ref, slo_ref,
     p_ref, kd_ref, vt_ref, h_even, h_odd) = rest[1 if split else 0:]
    i = pl.program_id(0)

    def next_rows(rows):
        if split:
            return jnp.where(i + 1 >= n_lat_tiles, xctx_ref[rows, :], xn_ref[rows, :])
        return xn_ref[rows, :]

    @pl.when(i == 0)
    def _():
        h_even[...] = _norm_mod(x0_ref[...], g_ref[...], m0_ref[0], 0, 1)

    for parity, (h_cur, h_nxt) in enumerate(((h_even, h_odd), (h_odd, h_even))):
        @pl.when(i % 2 == parity)
        def _():
            _inproj_body(h_cur, h_nxt, next_rows, g_ref, mn_ref, w_ref, cos_ref, shi_ref, slo_ref,
                         p_ref, kd_ref, vt_ref)


def _inproj(x, x_ctx, g, mods, w, layer, tabs, geo):
    n_tiles, tpb, n_lat_tiles, n_groups = geo
    split = x_ctx is not None
    nxt = lambda i: jnp.minimum(i + 1, n_tiles - 1)
    x_nxt = (lambda i: (jnp.minimum(i + 1, n_lat_tiles - 1), 0)) if split else (lambda i: (nxt(i), 0))
    group_nxt = lambda i: (jnp.minimum(nxt(i) // tpb, n_groups - 1), 0, 0)
    tab_idx = lambda i: (jnp.where(i < n_lat_tiles, i % tpb, tpb), 0)
    once = pl.Buffered(1)
    rows = n_tiles * TM
    return pl.pallas_call(
        functools.partial(_inproj_kernel, n_lat_tiles=n_lat_tiles, split=split),
        grid=(n_tiles,),
        in_specs=[
            pl.BlockSpec((TM, D_MODEL), lambda i: (0, 0), pipeline_mode=once),
            pl.BlockSpec((TM, D_MODEL), x_nxt),
        ] + ([pl.BlockSpec((TM, D_MODEL), lambda i: (0, 0), pipeline_mode=once)] if split else []) + [
            pl.BlockSpec((1, D_MODEL), lambda i: (0, 0)),
            pl.BlockSpec((1, N_MOD, D_MODEL), lambda i: (0, 0, 0)),
            pl.BlockSpec((1, N_MOD, D_MODEL), group_nxt),
            pl.BlockSpec((None, D_MODEL, D_IN_PROJ), lambda i: (layer, 0, 0), pipeline_mode=once),
            pl.BlockSpec((TM, LANES), tab_idx),
            pl.BlockSpec((TM, LANES), tab_idx),
            pl.BlockSpec((TM, LANES), tab_idx),
        ],
        out_specs=[
            pl.BlockSpec((TM, KV_START), lambda i: (i, 0)),
            pl.BlockSpec((TM, D_KDUP), lambda i: (i, 0)),
            pl.BlockSpec((D_KV, TM), lambda i: (0, i)),
        ],
        out_shape=[
            jax.ShapeDtypeStruct((rows, KV_START), BF16),
            jax.ShapeDtypeStruct((rows, D_KDUP), BF16),
            jax.ShapeDtypeStruct((D_KV, rows), BF16),
        ],
        scratch_shapes=[pltpu.VMEM((TM, D_MODEL), BF16), pltpu.VMEM((TM, D_MODEL), BF16)],
        compiler_params=_params(("arbitrary",)),
        name="inproj",
    )(x, x, *([x_ctx] if split else []), g, mods, mods, w, *tabs)


def _band_bias(first, last):
    jj = lax.broadcasted_iota(jnp.int32, (3 * BLOCK, GQA_GROUP * BLOCK), 0)
    rr = lax.broadcasted_iota(jnp.int32, (3 * BLOCK, GQA_GROUP * BLOCK), 1) % BLOCK
    ok = (jj >= rr) & (jj <= rr + 2 * WINDOW)
    if first:
        ok = ok & (jj >= BLOCK)
    if last:
        ok = ok & (jj < 2 * BLOCK)
    bias = jnp.where(ok, 0.0, NEG_INF).astype(F32)
    return bias[:BLOCK], bias[2 * BLOCK:]


def _attend_t(q_blk, head_masks, parts, snk):
    return _mix_values(*_scores(q_blk, head_masks, parts, snk), parts, snk)


def _scores(q_blk, head_masks, parts, snk):
    q_bd = jnp.concatenate([q_blk * hm for hm in head_masks], axis=0)
    s = _dot_nt(jnp.concatenate([k4 for k4, _, _ in parts], axis=0), q_bd)
    rows, r0 = [], 0
    for k4, _, bias in parts:
        n = k4.shape[0]
        if bias is None:
            rows.append(s[r0:r0 + n])
        else:
            rows += [s[r0:r0 + BLOCK] + bias[0], s[r0 + BLOCK:r0 + n - BLOCK], s[r0 + n - BLOCK:r0 + n] + bias[1]]
        r0 += n
    s = jnp.concatenate(rows, axis=0)
    return s, jnp.maximum(snk, jnp.max(s, axis=0, keepdims=True))


def _mix_values(s, m, parts, snk):
    acc = _dot(jnp.concatenate([vt for _, vt, _ in parts], axis=1), jnp.exp2(s - m).astype(BF16))
    denom = acc[HEAD_DIM:HEAD_DIM + 1, :] + jnp.exp2(snk - m)
    return acc[:HEAD_DIM, :] / denom


def _set_row(a, row, value):
    g0 = row // MOD_ROWS * MOD_ROWS
    r8 = lax.broadcasted_iota(jnp.int32, (MOD_ROWS, 1), 0)
    piece = jnp.where(r8 == row - g0, value, a[g0:g0 + MOD_ROWS])
    return jnp.concatenate([t for t in (a[:g0], piece, a[g0 + MOD_ROWS:]) if t.shape[0]], axis=0)


def _mixer_kernel(x_ref, p_ref, pp_ref, pn_ref,
                  kp_ref, kc_ref, kn_ref, vp_ref, vc_ref, vn_ref, ck_ref, cv_ref, *rest,
                  n_lat_tiles, tpb, n_ctx, n_seq_ctx, split):
    ctx_refs = rest[:2 * n_seq_ctx]
    rest = rest[2 * n_seq_ctx:]
    xctx_ref = rest[0] if split else x_ref
    (wout_ref, cw_ref, cb_ref, goc_ref, goa_ref, sink_ref, m_ref, o_ref, attn_scr, bias_scr) = rest[1 if split else 0:]
    i = pl.program_id(0)
    is_lat = i < n_lat_tiles
    tib = i % tpb

    @pl.when(i == 0)
    def _():
        for v, (first, last) in enumerate(((True, False), (False, False), (False, True))):
            lo, hi = _band_bias(first, last)
            bias_scr[v, 0] = lo
            bias_scr[v, 1] = hi

    lane_head = lax.broadcasted_iota(jnp.int32, (1, D_QG), 1) // HEAD_DIM
    head_masks = [(lane_head == g).astype(F32).astype(BF16) for g in range(GQA_GROUP)]
    ones_rows = jnp.ones((BF16_SUBLANES, TM + 2 * BLOCK), BF16)

    def conv_normed(lat):
        cg, hh = slice(P_CG, P_CG + D_CONV), slice(P_HH, P_HH + D_CONV)
        u = p_ref[:, cg].astype(F32) * p_ref[:, hh].astype(F32)
        u_prev, u_next = pltpu.roll(u, 1, 0), pltpu.roll(u, TM - 1, 0)
        if lat:
            last = BF16_SUBLANES - 1
            prev_edge = pp_ref[last:last + 1, cg].astype(F32) * pp_ref[last:last + 1, hh].astype(F32)
            next_edge = pn_ref[0:1, cg].astype(F32) * pn_ref[0:1, hh].astype(F32)
            u_prev = _set_row(u_prev, 0, jnp.where(tib > 0, prev_edge, 0.0))
            u_next = _set_row(u_next, TM - 1, jnp.where(tib < tpb - 1, next_edge, 0.0))
        else:
            for sb in range(n_seq_ctx):
                u_prev = _set_row(u_prev, sb * n_ctx, 0.0)
                u_next = _set_row(u_next, (sb + 1) * n_ctx - 1, 0.0)
        conv = u_prev * cw_ref[0:1, :] + u * cw_ref[1:2, :] + u_next * cw_ref[2:3, :] + cb_ref[...]
        return (_rms(p_ref[:, P_BG:P_BG + D_CONV].astype(F32) * conv) * goc_ref[...]).astype(BF16)

    def sink_row(kvh):
        return jnp.concatenate(
            [jnp.full((1, BLOCK), sink_ref[kvh * GQA_GROUP + g] * LOG2E, F32) for g in range(GQA_GROUP)], axis=1)

    def keys4(kd, kvh):
        k2 = kd[:, kvh * LANES:(kvh + 1) * LANES]
        return jnp.concatenate([k2] * (D_QG // LANES), axis=1)

    def values_t(vt, kvh):
        return jnp.concatenate([vt[kvh * HEAD_DIM:(kvh + 1) * HEAD_DIM, :], ones_rows[:, :vt.shape[1]]], axis=0)

    def put(kvh, qb, o_t):
        for g in range(GQA_GROUP):
            r0 = (kvh * GQA_GROUP + g) * HEAD_DIM
            attn_scr[r0:r0 + HEAD_DIM, qb * BLOCK:(qb + 1) * BLOCK] = o_t[:, g * BLOCK:(g + 1) * BLOCK]

    def run_units(units, extras):
        extras = list(extras)
        every = max(1, (len(units) - EXTRAS_FROM) // max(1, len(extras)))
        in_flight = []
        for u, unit in enumerate(units + [None] * UNITS_AHEAD):
            if unit is not None:
                kvh, qb, parts, snk = unit
                q_blk = p_ref[qb * BLOCK:(qb + 1) * BLOCK, P_Q + kvh * D_QG:P_Q + (kvh + 1) * D_QG]
                in_flight.append((kvh, qb, parts, snk, _scores(q_blk, head_masks, parts, snk)))
            if u >= UNITS_AHEAD:
                kvh, qb, parts, snk, (s, m) = in_flight.pop(0)
                put(kvh, qb, _mix_values(s, m, parts, snk))
            if extras and u >= EXTRAS_FROM and (u - EXTRAS_FROM) % every == 0:
                extras.pop(0)()
        for extra in extras:
            extra()

    def attend_latent(extras):
        kd_ext = jnp.concatenate([kp_ref[...], kc_ref[...], kn_ref[...]], axis=0)
        vt_ext = jnp.concatenate([vp_ref[...], vc_ref[...], vn_ref[...]], axis=1)
        first_idx = jnp.where(tib == 0, 0, 1)
        last_idx = jnp.where(tib == tpb - 1, 2, 1)
        units = []
        for kvh in range(N_KV_HEADS):
            k4_ext, vt4_ext = keys4(kd_ext, kvh), values_t(vt_ext, kvh)
            k4_ctx, vt4_ctx = keys4(ck_ref[...], kvh), values_t(cv_ref[...], kvh)
            snk = sink_row(kvh)
            for qb in range(N_QB):
                v = first_idx if qb == 0 else last_idx if qb == N_QB - 1 else 1
                win = slice(qb * BLOCK, qb * BLOCK + 3 * BLOCK)
                units.append((kvh, qb, [(k4_ext[win], vt4_ext[:, win], (bias_scr[v, 0], bias_scr[v, 1])),
                                        (k4_ctx, vt4_ctx, None)], snk))
        run_units(units, extras)

    def attend_context(extras):
        units = []
        for sb in range(n_seq_ctx):
            k_all, vt_all = ctx_refs[2 * sb][...], ctx_refs[2 * sb + 1][...]
            for kvh in range(N_KV_HEADS):
                k4, vt4 = keys4(k_all, kvh), values_t(vt_all, kvh)
                snk = sink_row(kvh)
                for qb in range(sb * n_ctx // BLOCK, (sb + 1) * n_ctx // BLOCK):
                    units.append((kvh, qb, [(k4, vt4, None)], snk))
        run_units(units, extras)

    def branch(lat):
        gate = m_ref[0, 2:3, :]
        xr_ref = x_ref if lat else xctx_ref
        conv_n = []

        def conv_proj(c):
            cols = slice(c * TN_IN, (c + 1) * TN_IN)
            o_ref[:, cols] = xr_ref[:, cols] + gate[:, cols] * _dot(conv_n[0], wout_ref[:D_CONV, cols])

        extras = [lambda: conv_n.append(conv_normed(lat))]
        extras += [functools.partial(conv_proj, c) for c in range(D_MODEL // TN_IN)]
        if lat:
            attend_latent(extras)
        else:
            attend_context(extras)
        a_t = attn_scr[...]
        rstd = lax.rsqrt(jnp.mean(a_t * a_t, axis=0, keepdims=True) + EPS)
        goa_t = jnp.concatenate([goa_ref[...]] * (TM // LANES), axis=1)
        attn_n = (a_t * rstd * goa_t).T.astype(BF16)
        o_ref[...] += gate * _dot(attn_n, wout_ref[D_CONV:, :])

    pl.when(is_lat)(functools.partial(branch, True))
    pl.when(jnp.logical_not(is_lat))(functools.partial(branch, False))


def _mixer(x, x_ctx, p, kd, vt, w_out, layer, conv_w, conv_b, g_oc, g_oa_t, sink, mods, geo, n_ctx, n_out_tiles):
    n_tiles, tpb, n_lat_tiles, n_groups = geo
    split = x_ctx is not None
    x_idx = (lambda i: (jnp.minimum(i, n_lat_tiles - 1), 0)) if split else (lambda i: (i, 0))
    n_seq_ctx = TM // n_ctx
    halo_blocks = TM // BF16_SUBLANES
    n_halo, n_kvb = n_tiles * halo_blocks, n_tiles * N_QB
    ctx_blk0 = n_lat_tiles * TM // n_ctx
    seq_of = lambda i: jnp.minimum(i // tpb, n_seq_ctx - 1)
    group = lambda i: (jnp.minimum(i // tpb, n_groups - 1), 0, 0)
    prev_blk = lambda i: jnp.maximum(i * N_QB - 1, 0)
    next_blk = lambda i: jnp.minimum((i + 1) * N_QB, n_kvb - 1)

    halo_prev = pl.BlockSpec((BF16_SUBLANES, 2 * D_CONV), lambda i: (jnp.maximum(i * halo_blocks - 1, 0), 0))
    halo_next = pl.BlockSpec((BF16_SUBLANES, 2 * D_CONV),
                             lambda i: (jnp.minimum((i + 1) * halo_blocks, n_halo - 1), 0))
    in_specs = [
        pl.BlockSpec((TM, D_MODEL), x_idx),
        pl.BlockSpec((TM, KV_START), lambda i: (i, 0)),
        halo_prev, halo_next,
        pl.BlockSpec((BLOCK, D_KDUP), lambda i: (prev_blk(i), 0)),
        pl.BlockSpec((TM, D_KDUP), lambda i: (i, 0)),
        pl.BlockSpec((BLOCK, D_KDUP), lambda i: (next_blk(i), 0)),
        pl.BlockSpec((D_KV, BLOCK), lambda i: (0, prev_blk(i))),
        pl.BlockSpec((D_KV, TM), lambda i: (0, i)),
        pl.BlockSpec((D_KV, BLOCK), lambda i: (0, next_blk(i))),
        pl.BlockSpec((n_ctx, D_KDUP), lambda i: (ctx_blk0 + seq_of(i), 0)),
        pl.BlockSpec((D_KV, n_ctx), lambda i: (0, ctx_blk0 + seq_of(i))),
    ]
    args = [x, p, p, p, kd, kd, kd, vt, vt, vt, kd, vt]
    for sb in range(n_seq_ctx):
        in_specs += [pl.BlockSpec((n_ctx, D_KDUP), functools.partial(lambda i, s: (ctx_blk0 + s, 0), s=sb)),
                     pl.BlockSpec((D_KV, n_ctx), functools.partial(lambda i, s: (0, ctx_blk0 + s), s=sb))]
        args += [kd, vt]
    if split:
        in_specs.append(pl.BlockSpec((TM, D_MODEL), lambda i: (0, 0), pipeline_mode=pl.Buffered(1)))
        args.append(x_ctx)
    in_specs += [
        pl.BlockSpec((None, D_MODEL, D_MODEL), lambda i: (layer, 0, 0), pipeline_mode=pl.Buffered(1)),
        pl.BlockSpec((3, D_CONV), lambda i: (0, 0)),
        pl.BlockSpec((1, D_CONV), lambda i: (0, 0)),
        pl.BlockSpec((1, D_CONV), lambda i: (0, 0)),
        pl.BlockSpec((D_ATTN, LANES), lambda i: (0, 0)),
        pl.BlockSpec(memory_space=pltpu.SMEM),
        pl.BlockSpec((1, N_MOD, D_MODEL), group),
    ]
    args += [w_out, conv_w, conv_b, g_oc, g_oa_t, sink, mods]
    kern = functools.partial(_mixer_kernel, n_lat_tiles=n_lat_tiles, tpb=tpb, n_ctx=n_ctx, n_seq_ctx=n_seq_ctx,
                             split=split)
    return pl.pallas_call(
        kern,
        grid=(n_out_tiles,),
        in_specs=in_specs,
        out_specs=pl.BlockSpec((TM, D_MODEL), lambda i: (i, 0)),
        out_shape=jax.ShapeDtypeStruct((n_out_tiles * TM, D_MODEL), F32),
        scratch_shapes=[pltpu.VMEM((D_ATTN, TM), F32),
                        pltpu.VMEM((3, 2, BLOCK, GQA_GROUP * BLOCK), F32)],
        compiler_params=_params(("arbitrary",)),
        name="mixer",
    )(*args)


def _mlp_kernel(xc_ref, xn_ref, g_ref, mc_ref, mn_ref, w1_ref, w2_ref, gf_ref, o_ref, h_even, h_odd, *, final):
    i, f = pl.program_id(0), pl.program_id(1)
    n_f = D_FF // TF
    rows_per = TM // n_f

    @pl.when((i == 0) & (f == 0))
    def _():
        h_even[...] = _norm_mod(xc_ref[...], g_ref[...], mc_ref[0], 3, 4)

    def step(h_cur, h_nxt, first):
        r0 = pl.multiple_of(f * rows_per, rows_per)
        h_nxt[pl.ds(r0, rows_per), :] = _norm_mod(xn_ref[pl.ds(r0, rows_per), :], g_ref[...], mn_ref[0], 3, 4)
        hid = jnp.square(jnp.maximum(_dot(h_cur[...], w1_ref[...]), 0.0)).astype(BF16)
        upd = mc_ref[0, 5:6, :] * _dot(hid, w2_ref[...])
        if first:
            o_ref[...] = xc_ref[...] + upd
        else:
            o_ref[...] += upd

    for parity, (h_cur, h_nxt) in enumerate(((h_even, h_odd), (h_odd, h_even))):
        pl.when((i % 2 == parity) & (f == 0))(functools.partial(step, h_cur, h_nxt, True))
        pl.when((i % 2 == parity) & (f > 0))(functools.partial(step, h_cur, h_nxt, False))

    if final:
        @pl.when(f == n_f - 1)
        def _():
            o_ref[...] = _rms(o_ref[...]) * gf_ref[...]


def _mlp(x, g, mods, w1, w2, layer, g_final, geo, n_out_tiles, final):
    n_tiles, tpb, n_lat_tiles, n_groups = geo
    nxt = lambda i: jnp.minimum(i + 1, n_out_tiles - 1)
    group = lambda i, f: (jnp.minimum(i // tpb, n_groups - 1), 0, 0)
    group_nxt = lambda i, f: (jnp.minimum(nxt(i) // tpb, n_groups - 1), 0, 0)
    return pl.pallas_call(
        functools.partial(_mlp_kernel, final=final),
        grid=(n_out_tiles, D_FF // TF),
        in_specs=[
            pl.BlockSpec((TM, D_MODEL), lambda i, f: (i, 0)),
            pl.BlockSpec((TM, D_MODEL), lambda i, f: (nxt(i), 0)),
            pl.BlockSpec((1, D_MODEL), lambda i, f: (0, 0)),
            pl.BlockSpec((1, N_MOD, D_MODEL), group),
            pl.BlockSpec((1, N_MOD, D_MODEL), group_nxt),
            pl.BlockSpec((None, D_MODEL, TF), lambda i, f: (layer, 0, f)),
            pl.BlockSpec((None, TF, D_MODEL), lambda i, f: (layer, f, 0)),
            pl.BlockSpec((1, D_MODEL), lambda i, f: (0, 0)),
        ],
        out_specs=pl.BlockSpec((TM, D_MODEL), lambda i, f: (i, 0)),
        out_shape=jax.ShapeDtypeStruct((n_out_tiles * TM, D_MODEL), F32),
        scratch_shapes=[pltpu.VMEM((TM, D_MODEL), BF16), pltpu.VMEM((TM, D_MODEL), BF16)],
        compiler_params=_params(("arbitrary", "arbitrary")),
        name="mlp",
    )(x, x, g, mods, mods, w1, w2, g_final)


def _rope_tables(n_tok):
    rows = n_tok // GRID_W
    row_pos = jnp.repeat(jnp.arange(rows, dtype=F32), GRID_W)
    col_pos = jnp.tile(jnp.arange(GRID_W, dtype=F32), rows)
    inv = ROPE_THETA ** (-jnp.arange(0, ROPE_AXIS_DIM, 2, dtype=F32) / ROPE_AXIS_DIM)
    ang_r = row_pos[:, None] * inv[None, :]
    ang_c = col_pos[:, None] * inv[None, :]
    zeros = jnp.zeros_like(ang_r)
    cos_h = jnp.concatenate([jnp.cos(ang_r)] * 2 + [jnp.cos(ang_c)] * 2, axis=-1)
    hi_h = jnp.concatenate([zeros, jnp.sin(ang_r), zeros, jnp.sin(ang_c)], axis=-1)
    lo_h = jnp.concatenate([-jnp.sin(ang_r), zeros, -jnp.sin(ang_c), zeros], axis=-1)
    reps = LANES // HEAD_DIM
    pad = lambda t, fill: jnp.concatenate([jnp.tile(t, (1, reps)), jnp.full((TM, LANES), fill, F32)], axis=0)
    return pad(cos_h, 1.0), pad(hi_h, 0.0), pad(lo_h, 0.0)


def kernel(x, c, ctx, c_ctx, w_ada, b_ada, g_norm1, g_norm2, w_in, conv_w, conv_b, sink,
           g_out_conv, g_out_attn, w_out, w_mlp1, w_mlp2, g_final):
    bsz, n_tok, d = x.shape
    n_ctx = ctx.shape[1]
    depth = w_ada.shape[0]
    assert d == D_MODEL and n_tok % TM == 0 and n_tok % GRID_W == 0
    assert bsz * n_ctx == TM and n_ctx % BLOCK == 0 and bsz + 1 <= MOD_ROWS
    tpb = n_tok // TM
    n_lat_tiles = bsz * tpb
    n_tiles = n_lat_tiles + 1
    geo = (n_tiles, tpb, n_lat_tiles, bsz + 1)

    cc = jnp.concatenate([c, c_ctx[None, :], jnp.zeros((MOD_ROWS - bsz - 1, d), F32)], axis=0)
    mods = _ada(cc, w_ada, b_ada).reshape(depth, MOD_ROWS, N_MOD, d)
    tabs = _rope_tables(n_tok)
    xs, xs_ctx = x.reshape(bsz * n_tok, d), ctx.reshape(bsz * n_ctx, d)
    row = lambda v: v.reshape(1, -1)
    w_in, w_out, w_mlp1, w_mlp2 = (w.astype(BF16) for w in (w_in, w_out, w_mlp1, w_mlp2))

    for i in range(depth):
        last = i == depth - 1
        n_out = n_lat_tiles if last else n_tiles
        p, kd, vt = _inproj(xs, xs_ctx, row(g_norm1[i]), mods[i], w_in, i, tabs, geo)
        g_oa_t = jnp.broadcast_to(g_out_attn[i][:, None], (D_ATTN, LANES))
        xs = _mixer(xs, xs_ctx, p, kd, vt, w_out, i, conv_w[i], row(conv_b[i]), row(g_out_conv[i]),
                    g_oa_t, sink[i], mods[i], geo, n_ctx, n_out)
        xs_ctx = None
        xs = _mlp(xs, row(g_norm2[i]), mods[i], w_mlp1, w_mlp2, i, row(g_final), geo, n_out, last)
    return xs.reshape(bsz, n_tok, d)
```
